```python
import math
import jax, jax.numpy as jnp
from jax import lax
import numpy as np

D_MODEL = 1024
BATCH = 16
SEQ = 4096
DEPTH = 2
DEC_BATCH = 8
DEC_SEQ = 64
PAST_LEN = 1024

CHUNK = 64
Q_BLOCK = 128
DSA_Q_BLOCK = 32
NORM_EPS = 1e-6
SUBLN_EPS = 1e-5
N_EVEN = (DEPTH + 1) // 2
N_ODD = DEPTH // 2

MLA_HEADS = 8
MLA_NOPE = 128
MLA_ROPE = 64
MLA_V = 128
Q_LORA = 384
KV_LORA = 256
ROPE_THETA = 10000.0
MLA_SCALE = (MLA_NOPE + MLA_ROPE) ** -0.5

DIFF_HEADS = 8
DIFF_DIM = 64
DIFF_SCALE = DIFF_DIM ** -0.5
DIFF_QK = DIFF_HEADS * 2 * DIFF_DIM

DSA_HEADS = 8
DSA_DIM = 128
DSA_W = DSA_HEADS * DSA_DIM
DSA_SCALE = DSA_DIM ** -0.5
IDX_HEADS = 8
IDX_DIM = 64
IDX_SCALE = IDX_DIM ** -0.5 * IDX_HEADS ** -0.5
DSA_TOPK = 256

NUM_BUCKETS = 32
MAX_DISTANCE = 128
REL_HEADS = 8

D_FF = -(-8 * D_MODEL // (3 * 256)) * 256

IN_EVEN = Q_LORA + KV_LORA + MLA_ROPE + 3 * DIFF_QK
SPLIT_EVEN = [Q_LORA, Q_LORA + KV_LORA, Q_LORA + KV_LORA + MLA_ROPE,
              Q_LORA + KV_LORA + MLA_ROPE + DIFF_QK, Q_LORA + KV_LORA + MLA_ROPE + 2 * DIFF_QK]
MIX_EVEN = MLA_HEADS * MLA_V + DIFF_HEADS * 2 * DIFF_DIM
IN_ODD = 3 * DSA_W + IDX_HEADS * IDX_DIM + IDX_DIM + IDX_HEADS
SPLIT_ODD = [DSA_W, 2 * DSA_W, 3 * DSA_W, 3 * DSA_W + IDX_HEADS * IDX_DIM,
             3 * DSA_W + IDX_HEADS * IDX_DIM + IDX_DIM]

kernel_name = 'hybrid_mla_diff_dsa_streaming_step'


def _rmsnorm(x, g, eps=NORM_EPS):
    xf = x.astype(jnp.float32)
    y = xf * lax.rsqrt(jnp.mean(xf * xf, axis=-1, keepdims=True) + eps)
    return (y * g.astype(jnp.float32)).astype(x.dtype)


def _rope(x, pos):
    half = x.shape[-1] // 2
    inv = ROPE_THETA ** (-jnp.arange(half, dtype=jnp.float32) / half)
    ang = pos.astype(jnp.float32)[:, None] * inv[None, :]
    ang = ang.reshape((pos.shape[0],) + (1,) * (x.ndim - 3) + (half,))
    cos, sin = jnp.cos(ang), jnp.sin(ang)
    x1 = x[..., :half].astype(jnp.float32)
    x2 = x[..., half:].astype(jnp.float32)
    return jnp.concatenate([x1 * cos - x2 * sin, x2 * cos + x1 * sin], axis=-1).astype(x.dtype)


def _chunk_mask(q_pos, k_pos):
    return (k_pos[None, :] // CHUNK) <= (q_pos[:, None] // CHUNK)


def _t5_bucket(rel):
    nb = NUM_BUCKETS // 2
    max_exact = nb // 2
    ret = jnp.where(rel > 0, nb, 0)
    n = jnp.abs(rel)
    nf = jnp.maximum(n, 1).astype(jnp.float32)
    large = max_exact + (jnp.log(nf / max_exact) / math.log(MAX_DISTANCE / max_exact)
                         * (nb - max_exact)).astype(jnp.int32)
    large = jnp.minimum(large, nb - 1)
    return ret + jnp.where(n < max_exact, n, large)


def _rel_bias(rel, table):
    return table[_t5_bucket(rel)].astype(jnp.float32)


def _sweep(attend, q_pos, q_args, block):
    s = q_pos.shape[0]
    if s <= block or s % block:
        return attend(q_pos, *q_args)

    def one(i):
        start = i * block
        qp = lax.dynamic_slice_in_dim(q_pos, start, block)
        qs = [lax.dynamic_slice_in_dim(a, start, block, axis=1) for a in q_args]
        return attend(qp, *qs)

    out = jnp.moveaxis(lax.map(one, jnp.arange(s // block)), 0, 1)
    return out.reshape((out.shape[0], s) + out.shape[3:])


def _mla_attend(q_pos, q_nope, q_pe, k_nope, k_pe, v, k_pos):
    logits = (jnp.einsum('bqhd,bkhd->bhqk', q_nope, k_nope)
              + jnp.einsum('bqhr,bkr->bhqk', q_pe, k_pe)).astype(jnp.float32) * MLA_SCALE
    logits = jnp.where(_chunk_mask(q_pos, k_pos), logits, -jnp.inf)
    p = jax.nn.softmax(logits, axis=-1).astype(v.dtype)
    return jnp.einsum('bhqk,bkhd->bqhd', p, v)


def _diff_attend(q_pos, q, k, v, k_pos, rel_table, lam):
    k2 = k.reshape(k.shape[:3] + (2, DIFF_DIM))
    bias = jnp.moveaxis(_rel_bias(k_pos[None, :] - q_pos[:, None], rel_table), -1, 0)
    logits = jnp.einsum('bqhmd,bkhmd->bmhqk', q, k2).astype(jnp.float32) * DIFF_SCALE + bias
    logits = jnp.where(_chunk_mask(q_pos, k_pos), logits, -jnp.inf)
    p = jax.nn.softmax(logits, axis=-1)
    attn = (p[:, 0] - lam * p[:, 1]).astype(v.dtype)
    return jnp.einsum('bhqk,bkhd->bqhd', attn, v)


def _dsa_attend(q_pos, q, q_idx, w_idx, k, v, k_idx, k_pos, rel_table, top_k):
    dots = jnp.einsum('bqhd,bkd->bqhk', q_idx, k_idx)
    score = jnp.einsum('bqh,bqhk->bqk', w_idx, jax.nn.relu(dots)).astype(jnp.float32) * IDX_SCALE
    score = jnp.where(_chunk_mask(q_pos, k_pos), score, -jnp.inf)
    top_val, top_idx = lax.top_k(score, top_k)
    valid = top_val > -jnp.inf
    gather = jax.vmap(lambda rows, idx: rows[idx])
    k_sel = gather(k, top_idx)
    v_sel = gather(v, top_idx)
    rel = k_pos[top_idx] - q_pos[None, :, None]
    bias = jnp.moveaxis(_rel_bias(rel, rel_table), -1, 1)
    logits = jnp.einsum('bqhd,bqkhd->bhqk', q, k_sel).astype(jnp.float32) * DSA_SCALE + bias
    logits = jnp.where(valid[:, None], logits, -jnp.inf)
    p = jax.nn.softmax(logits, axis=-1).astype(v.dtype)
    return jnp.einsum('bhqk,bqkhd->bqhd', p, v_sel)


def _even_mixer(h, pos, past, w_in, g_q_lora, w_q_up, g_kv_lora, w_kv_up,
                lam_q1, lam_k1, lam_q2, lam_k2, g_subln, w_out, rel_table, lam_init):
    b, s, _ = h.shape
    q_lat, kv_lat, kpe_raw, dq, dk, dv = jnp.split(h @ w_in, SPLIT_EVEN, axis=-1)
    q = (_rmsnorm(q_lat, g_q_lora) @ w_q_up).reshape(b, s, MLA_HEADS, MLA_NOPE + MLA_ROPE)
    q_nope, q_pe = q[..., :MLA_NOPE], _rope(q[..., MLA_NOPE:], pos)
    ckv = _rmsnorm(kv_lat, g_kv_lora)
    kpe = _rope(kpe_raw, pos)
    dq = dq.reshape(b, s, DIFF_HEADS, 2, DIFF_DIM)
    dk = dk.reshape(b, s, DIFF_HEADS, 2 * DIFF_DIM)
    dv = dv.reshape(b, s, DIFF_HEADS, 2 * DIFF_DIM)
    new = (ckv, kpe, dk, dv)
    if past is None:
        ckv_all, kpe_all, dk_all, dv_all = new
    else:
        ckv_all, kpe_all, dk_all, dv_all = [jnp.concatenate([c, n], axis=1) for c, n in zip(past, new)]
    n_keys = ckv_all.shape[1]
    k_pos = jnp.arange(n_keys, dtype=jnp.int32)
    kv = (ckv_all @ w_kv_up).reshape(b, n_keys, MLA_HEADS, MLA_NOPE + MLA_V)
    k_nope, v_mla = kv[..., :MLA_NOPE], kv[..., MLA_NOPE:]
    a_out = _sweep(lambda qp, qn, qr: _mla_attend(qp, qn, qr, k_nope, kpe_all, v_mla, k_pos),
                   pos, (q_nope, q_pe), Q_BLOCK)
    lam = (jnp.exp(jnp.sum(lam_q1.astype(jnp.float32) * lam_k1.astype(jnp.float32)))
           - jnp.exp(jnp.sum(lam_q2.astype(jnp.float32) * lam_k2.astype(jnp.float32))) + lam_init)
    b_out = _sweep(lambda qp, qd: _diff_attend(qp, qd, dk_all, dv_all, k_pos, rel_table, lam),
                   pos, (dq,), Q_BLOCK)
    b_out = _rmsnorm(b_out, g_subln, SUBLN_EPS) * (1.0 - lam_init)
    mixed = jnp.concatenate([a_out.reshape(b, s, -1), b_out.reshape(b, s, -1)], axis=-1)
    return mixed @ w_out, new


def _odd_mixer(h, pos, past, w_in, w_out, rel_table):
    b, s, _ = h.shape
    q, k, v, q_idx, k_idx, w_idx = jnp.split(h @ w_in, SPLIT_ODD, axis=-1)
    q = q.reshape(b, s, DSA_HEADS, DSA_DIM)
    k = k.reshape(b, s, DSA_HEADS, DSA_DIM)
    v = v.reshape(b, s, DSA_HEADS, DSA_DIM)
    q_idx = q_idx.reshape(b, s, IDX_HEADS, IDX_DIM)
    new = (k, v, k_idx)
    if past is None:
        k_all, v_all, kidx_all = new
    else:
        k_all, v_all, kidx_all = [jnp.concatenate([c, n], axis=1) for c, n in zip(past, new)]
    n_keys = k_all.shape[1]
    k_pos = jnp.arange(n_keys, dtype=jnp.int32)
    top_k = min(DSA_TOPK, n_keys // 4)
    out = _sweep(lambda qp, qq, qi, wi: _dsa_attend(qp, qq, qi, wi, k_all, v_all, kidx_all, k_pos,
                                                     rel_table, top_k),
                 pos, (q, q_idx, w_idx), DSA_Q_BLOCK)
    return out.reshape(b, s, -1) @ w_out, new


def _swiglu(h, w_gate, w_up, w_down):
    return (jax.nn.silu(h @ w_gate) * (h @ w_up)) @ w_down


def setup_inputs(seed: int = 0) -> dict:
    key = jax.random.key(seed)
    ks = iter(jax.random.split(key, 40))

    def nrm(shape, scale):
        return jax.random.normal(next(ks), shape, jnp.float32) * scale

    def gain(shape):
        return 1.0 + 0.02 * jax.random.normal(next(ks), shape, jnp.float32)

    return {
        'x_prompt': nrm((BATCH, SEQ, D_MODEL), 1.0),
        'x_sample': nrm((DEC_BATCH, DEC_SEQ, D_MODEL), 1.0),
        'cache_mla_ckv': nrm((N_EVEN, DEC_BATCH, PAST_LEN, KV_LORA), 1.0),
        'cache_mla_kpe': nrm((N_EVEN, DEC_BATCH, PAST_LEN, MLA_ROPE), 1.0),
        'cache_diff_k': nrm((N_EVEN, DEC_BATCH, PAST_LEN, DIFF_HEADS, 2 * DIFF_DIM), 1.0),
        'cache_diff_v': nrm((N_EVEN, DEC_BATCH, PAST_LEN, DIFF_HEADS, 2 * DIFF_DIM), 1.0),
        'cache_dsa_k': nrm((N_ODD, DEC_BATCH, PAST_LEN, DSA_HEADS, DSA_DIM), 1.0),
        'cache_dsa_v': nrm((N_ODD, DEC_BATCH, PAST_LEN, DSA_HEADS, DSA_DIM), 1.0),
        'cache_dsa_kidx': nrm((N_ODD, DEC_BATCH, PAST_LEN, IDX_DIM), 1.0),
        'g_attn_even': gain((N_EVEN, D_MODEL)),
        'w_in_even': nrm((N_EVEN, D_MODEL, IN_EVEN), D_MODEL ** -0.5),
        'g_q_lora': gain((N_EVEN, Q_LORA)),
        'w_q_up': nrm((N_EVEN, Q_LORA, MLA_HEADS * (MLA_NOPE + MLA_ROPE)), Q_LORA ** -0.5),
        'g_kv_lora': gain((N_EVEN, KV_LORA)),
        'w_kv_up': nrm((N_EVEN, KV_LORA, MLA_HEADS * (MLA_NOPE + MLA_V)), KV_LORA ** -0.5),
        'lambda_q1': nrm((N_EVEN, DIFF_DIM), 0.1),
        'lambda_k1': nrm((N_EVEN, DIFF_DIM), 0.1),
        'lambda_q2': nrm((N_EVEN, DIFF_DIM), 0.1),
        'lambda_k2': nrm((N_EVEN, DIFF_DIM), 0.1),
        'g_diff_subln': gain((N_EVEN, 2 * DIFF_DIM)),
        'w_out_even': nrm((N_EVEN, MIX_EVEN, D_MODEL), MIX_EVEN ** -0.5),
        'g_attn_odd': gain((N_ODD, D_MODEL)),
        'w_in_odd': nrm((N_ODD, D_MODEL, IN_ODD), D_MODEL ** -0.5),
        'w_out_odd': nrm((N_ODD, DSA_W, D_MODEL), DSA_W ** -0.5),
        'rel_bias_table': nrm((NUM_BUCKETS, REL_HEADS), 0.2),
        'g_ffn': gain((DEPTH, D_MODEL)),
        'w_gate': nrm((DEPTH, D_MODEL, D_FF), D_MODEL ** -0.5),
        'w_up': nrm((DEPTH, D_MODEL, D_FF), D_MODEL ** -0.5),
        'w_down': nrm((DEPTH, D_FF, D_MODEL), D_FF ** -0.5),
        'g_final': gain((D_MODEL,)),
    }


def reference(x_prompt, x_sample, cache_mla_ckv, cache_mla_kpe, cache_diff_k, cache_diff_v,
              cache_dsa_k, cache_dsa_v, cache_dsa_kidx, g_attn_even, w_in_even, g_q_lora, w_q_up,
              g_kv_lora, w_kv_up, lambda_q1, lambda_k1, lambda_q2, lambda_k2, g_diff_subln,
              w_out_even, g_attn_odd, w_in_odd, w_out_odd, rel_bias_table, g_ffn, w_gate, w_up,
              w_down, g_final):
    xp, xs = x_prompt, x_sample
    past_len = cache_mla_ckv.shape[2]
    pos_p = jnp.arange(xp.shape[1], dtype=jnp.int32)
    pos_s = past_len + jnp.arange(xs.shape[1], dtype=jnp.int32)
    even_p, even_s, odd_p, odd_s = [], [], [], []
    for layer in range(DEPTH):
        if layer % 2 == 0:
            e = layer // 2
            lam_init = 0.8 - 0.6 * math.exp(-0.3 * layer)
            args = (w_in_even[e], g_q_lora[e], w_q_up[e], g_kv_lora[e], w_kv_up[e],
                    lambda_q1[e], lambda_k1[e], lambda_q2[e], lambda_k2[e], g_diff_subln[e],
                    w_out_even[e], rel_bias_table, lam_init)
            dp, st_p = _even_mixer(_rmsnorm(xp, g_attn_even[e]), pos_p, None, *args)
            past = (cache_mla_ckv[e], cache_mla_kpe[e], cache_diff_k[e], cache_diff_v[e])
            ds, st_s = _even_mixer(_rmsnorm(xs, g_attn_even[e]), pos_s, past, *args)
            even_p.append(st_p)
            even_s.append(st_s)
        else:
            o = layer // 2
            args = (w_in_odd[o], w_out_odd[o], rel_bias_table)
            dp, st_p = _odd_mixer(_rmsnorm(xp, g_attn_odd[o]), pos_p, None, *args)
            past = (cache_dsa_k[o], cache_dsa_v[o], cache_dsa_kidx[o])
            ds, st_s = _odd_mixer(_rmsnorm(xs, g_attn_odd[o]), pos_s, past, *args)
            odd_p.append(st_p)
            odd_s.append(st_s)
        xp = xp + dp
        xs = xs + ds
        xp = xp + _swiglu(_rmsnorm(xp, g_ffn[layer]), w_gate[layer], w_up[layer], w_down[layer])
        xs = xs + _swiglu(_rmsnorm(xs, g_ffn[layer]), w_gate[layer], w_up[layer], w_down[layer])
    y_prompt = _rmsnorm(xp, g_final)
    y_sample = _rmsnorm(xs, g_final)
    mla_ckv_p, mla_kpe_p, diff_k_p, diff_v_p = [jnp.stack(a) for a in zip(*even_p)]
    mla_ckv_s, mla_kpe_s, diff_k_s, diff_v_s = [jnp.stack(a) for a in zip(*even_s)]
    dsa_k_p, dsa_v_p, dsa_kidx_p = [jnp.stack(a) for a in zip(*odd_p)]
    dsa_k_s, dsa_v_s, dsa_kidx_s = [jnp.stack(a) for a in zip(*odd_s)]
    return (y_prompt, y_sample,
            mla_ckv_p, mla_kpe_p, diff_k_p, diff_v_p, dsa_k_p, dsa_v_p, dsa_kidx_p,
            mla_ckv_s, mla_kpe_s, diff_k_s, diff_v_s, dsa_k_s, dsa_v_s, dsa_kidx_s)
```

```python
import functools
import math

import jax
import jax.numpy as jnp
from jax import lax
from jax.experimental import pallas as pl
from jax.experimental.pallas import tpu as pltpu

D_MODEL = 1024
CHUNK = 64
NORM_EPS = 1e-6
SUBLN_EPS = 1e-5
HEADS = 8
MLA_NOPE = 128
MLA_ROPE = 64
MLA_V = 128
Q_LORA = 384
KV_LORA = 256
ROPE_THETA = 10000.0
MLA_SCALE = (MLA_NOPE + MLA_ROPE) ** -0.5
DIFF_DIM = 64
DIFF_SCALE = DIFF_DIM ** -0.5
DSA_DIM = 128
DSA_SCALE = DSA_DIM ** -0.5
IDX_DIM = 64
IDX_SCALE = IDX_DIM ** -0.5 * HEADS ** -0.5
DSA_TOPK = 256
NUM_BUCKETS = 32
T5_FAR = 91
HEAD_W = 128
MLA_QK = 256
D_FF = -(-8 * D_MODEL // (3 * 256)) * 256
FF_CHUNK = 256

LANES = 128
VMEM_LIMIT = 56 * 1024 * 1024

INT_MIN = -2 ** 31
NEG_BIG = -1e30
BF = jnp.bfloat16
F32 = jnp.float32


def _dot(a, b):
    return jnp.dot(a, b, preferred_element_type=F32)


def _dot_nt(a, b):
    return lax.dot_general(a, b, (((1,), (1,)), ((), ())), preferred_element_type=F32)


def _rms(x, g, eps):
    return x * lax.rsqrt(jnp.mean(x * x, axis=-1, keepdims=True) + eps) * g


def _params(sem):
    return pltpu.CompilerParams(dimension_semantics=sem, vmem_limit_bytes=VMEM_LIMIT)


def _const_spec(shape):
    nd = len(shape)
    return pl.BlockSpec(shape, lambda *_: (0,) * nd, pipeline_mode=pl.Buffered(1))


def _even_proj_kernel(x_ref, g_ref, wq_ref, wkv_ref, wd_ref, wkpe_ref, wkrot_ref, gq_ref,
                      wqcat_ref, wqrot_ref, gkv_ref, cos_ref, sin_ref,
                      qcat_ref, ckv_ref, kpe_ref, dq_ref, dk_ref, dv_ref, dk16_ref, dv16_ref):
    h = _rms(x_ref[...], g_ref[...], NORM_EPS).astype(BF)
    cos = cos_ref[...]
    sin = sin_ref[...]
    qn = _rms(_dot(h, wq_ref[...]), gq_ref[...], NORM_EPS).astype(BF)
    qc = _dot(qn, wqcat_ref[...])
    qr = _dot(qn, wqrot_ref[...])
    for hh in range(HEADS):
        a = hh * MLA_QK
        qcat_ref[:, a:a + HEAD_W] = (qc[:, a:a + HEAD_W] * MLA_SCALE).astype(BF)
        pe = qc[:, a + HEAD_W:a + MLA_QK] * cos + qr[:, hh * HEAD_W:(hh + 1) * HEAD_W] * sin
        qcat_ref[:, a + HEAD_W:a + MLA_QK] = (pe * MLA_SCALE).astype(BF)
    ckv_ref[...] = _rms(_dot(h, wkv_ref[...]), gkv_ref[...], NORM_EPS)
    kpe_ref[...] = (_dot(h, wkpe_ref[...]) * cos[:, :MLA_ROPE]
                    + _dot(h, wkrot_ref[...]) * sin[:, :MLA_ROPE])
    dq_ref[...] = (_dot(h, wd_ref[:, 0:1024]) * DIFF_SCALE).astype(BF)
    dk = _dot(h, wd_ref[:, 1024:2048])
    dk_ref[...] = dk
    dk16_ref[...] = dk.astype(BF)
    dv = _dot(h, wd_ref[:, 2048:3072])
    dv_ref[...] = dv
    dv16_ref[...] = dv.astype(BF)


def _even_proj(x2d, seq, tm, g_attn, w_in, g_q, w_q_up, g_kv, cos_p, sin_p):
    t = x2d.shape[0]
    c0, c1, c2 = Q_LORA, Q_LORA + KV_LORA, Q_LORA + KV_LORA + MLA_ROPE
    half = MLA_ROPE // 2
    wq = w_in[:, :c0].astype(BF)
    wkv = w_in[:, c0:c1].astype(BF)
    wkpe = w_in[:, c1:c2]
    wkrot = jnp.concatenate([-wkpe[:, half:], wkpe[:, :half]], axis=1).astype(BF)
    wkpe = wkpe.astype(BF)
    wd = w_in[:, c2:].astype(BF)
    wqu = w_q_up.reshape(Q_LORA, HEADS, MLA_NOPE + MLA_ROPE)
    w_nope, w_pe = wqu[..., :MLA_NOPE], wqu[..., MLA_NOPE:]
    w_rot = jnp.concatenate([-w_pe[..., half:], w_pe[..., :half]], axis=-1)
    zpad = jnp.zeros((Q_LORA, HEADS, MLA_QK - MLA_NOPE - MLA_ROPE), F32)
    wqcat = jnp.concatenate([w_nope, w_pe, zpad], axis=-1).reshape(Q_LORA, HEADS * MLA_QK).astype(BF)
    wqrot = jnp.concatenate([w_rot, zpad], axis=-1).reshape(Q_LORA, HEADS * HEAD_W).astype(BF)
    n_pos = seq // tm
    row = lambda w: pl.BlockSpec((tm, w), lambda i: (i, 0))
    pos = pl.BlockSpec((tm, HEAD_W), lambda i: (i % n_pos, 0))
    out_shapes = [
        jax.ShapeDtypeStruct((t, HEADS * MLA_QK), BF),
        jax.ShapeDtypeStruct((t, KV_LORA), F32),
        jax.ShapeDtypeStruct((t, MLA_ROPE), F32),
        jax.ShapeDtypeStruct((t, 1024), BF),
        jax.ShapeDtypeStruct((t, 1024), F32),
        jax.ShapeDtypeStruct((t, 1024), F32),
        jax.ShapeDtypeStruct((t, 1024), BF),
        jax.ShapeDtypeStruct((t, 1024), BF),
    ]
    ins = [x2d, g_attn.reshape(1, -1), wq, wkv, wd, wkpe, wkrot, g_q.reshape(1, -1), wqcat, wqrot,
           g_kv.reshape(1, -1), cos_p, sin_p]
    in_specs = [row(D_MODEL)] + [_const_spec(a.shape) for a in ins[1:11]] + [pos, pos]
    return pl.pallas_call(
        _even_proj_kernel,
        grid=(t // tm,),
        in_specs=in_specs,
        out_specs=[row(s.shape[1]) for s in out_shapes],
        out_shape=out_shapes,
        compiler_params=_params(("parallel",)),
        name="even_proj",
    )(*ins)


def _kv_up_kernel(ckv_ref, kpe_ref, wk_ref, wv_ref, kcat_ref, v_ref):
    c = ckv_ref[...].astype(BF)
    kn = _dot(c, wk_ref[...])
    v_ref[...] = _dot(c, wv_ref[...]).astype(BF)
    kpe = kpe_ref[...]
    kp = jnp.concatenate([kpe, jnp.zeros_like(kpe)], axis=-1).astype(BF)
    for hh in range(HEADS):
        a = hh * MLA_QK
        kcat_ref[:, a:a + HEAD_W] = kn[:, hh * HEAD_W:(hh + 1) * HEAD_W].astype(BF)
        kcat_ref[:, a + HEAD_W:a + MLA_QK] = kp


def _kv_up(ckv2d, kpe2d, w_kv_up, tm):
    t = ckv2d.shape[0]
    wkv = w_kv_up.reshape(KV_LORA, HEADS, MLA_NOPE + MLA_V)
    wk = wkv[..., :MLA_NOPE].reshape(KV_LORA, HEADS * MLA_NOPE).astype(BF)
    wv = wkv[..., MLA_NOPE:].reshape(KV_LORA, HEADS * MLA_V).astype(BF)
    row = lambda w: pl.BlockSpec((tm, w), lambda i: (i, 0))
    return pl.pallas_call(
        _kv_up_kernel,
        grid=(t // tm,),
        in_specs=[row(KV_LORA), row(MLA_ROPE), _const_spec(wk.shape), _const_spec(wv.shape)],
        out_specs=[row(HEADS * MLA_QK), row(HEADS * MLA_V)],
        out_shape=[jax.ShapeDtypeStruct((t, HEADS * MLA_QK), BF),
                   jax.ShapeDtypeStruct((t, HEADS * MLA_V), BF)],
        compiler_params=_params(("parallel",)),
        name="kv_up",
    )(ckv2d, kpe2d, wk, wv)


def _n_near(tk):
    return 1 + -(-(T5_FAR - 1) // tk)


def _bias_kernel(tab_ref, out_ref, *, tq, tk, n_near):
    h = pl.program_id(0)
    row = lax.broadcasted_iota(jnp.int32, (tq, tk), 0)
    col = lax.broadcasted_iota(jnp.int32, (tq, tk), 1)
    out_ref[0, 0] = jnp.full((tq, tk), tab_ref[NUM_BUCKETS // 2 - 1, h], F32)
    for n in range(1, n_near + 1):
        rel = col - row - (n_near - n) * tk
        dist = jnp.abs(rel)
        large = jnp.full((tq, tk), NUM_BUCKETS // 4, jnp.int32)
        for thr in (12, 16, 23, 32, 46, 64, T5_FAR):
            large = large + jnp.where(dist >= thr, 1, 0)
        bucket = jnp.where(rel > 0, NUM_BUCKETS // 2, 0) + jnp.where(dist < NUM_BUCKETS // 4, dist, large)
        val = jnp.zeros((tq, tk), F32)
        for b in range(NUM_BUCKETS):
            val = jnp.where(bucket == b, tab_ref[b, h], val)
        if n == n_near:
            val = jnp.where((col // CHUNK) <= (row // CHUNK), val, -jnp.inf)
        out_ref[0, n] = val


def _bias_tiles(rel_table, tq, tk):
    n_near = _n_near(tk)
    return pl.pallas_call(
        functools.partial(_bias_kernel, tq=tq, tk=tk, n_near=n_near),
        grid=(HEADS,),
        in_specs=[pl.BlockSpec(memory_space=pltpu.SMEM)],
        out_specs=pl.BlockSpec((1, n_near + 1, tq, tk), lambda h: (h, 0, 0, 0)),
        out_shape=jax.ShapeDtypeStruct((HEADS, n_near + 1, tq, tk), F32),
        compiler_params=_params(("arbitrary",)),
        name="bias_tiles",
    )(rel_table)


def _flash_kernel(*refs, tq, tk, past, n_maps, has_bias, has_sel, lam_init):
    refs = list(refs)
    q_ref, k_ref, v_ref = refs[:3]
    pos = 3
    bias_ref = sel_ref = lam_ref = gsub_ref = None
    if has_bias:
        bias_ref = refs[pos]; pos += 1
    if has_sel:
        sel_ref = refs[pos]; pos += 1
    if n_maps == 2:
        lam_ref, gsub_ref = refs[pos], refs[pos + 1]; pos += 2
    o_ref, m_ref, l_ref, acc_ref = refs[pos:pos + 4]
    n_near = _n_near(tk) if has_bias else 1

    j_diag = past // tk + pl.program_id(2)
    q = q_ref[0]
    if n_maps == 2:
        lane = lax.broadcasted_iota(jnp.int32, q.shape, 1)
        qs = [jnp.where(lane < DIFF_DIM, q, jnp.zeros_like(q)),
              jnp.where(lane >= DIFF_DIM, q, jnp.zeros_like(q))]
    else:
        qs = [q]
    m_ref[...] = jnp.full(m_ref.shape, NEG_BIG, F32)
    l_ref[...] = jnp.zeros(l_ref.shape, F32)
    acc_ref[...] = jnp.zeros(acc_ref.shape, F32)

    def tile(j, slot):
        start = pl.multiple_of(j * tk, tk)
        k = k_ref[0, pl.ds(start, tk), :]
        v = v_ref[0, pl.ds(start, tk), :]
        extra = None
        if has_bias:
            extra = bias_ref[0, slot]
        elif slot == n_near:
            row = lax.broadcasted_iota(jnp.int32, (tq, tk), 0)
            col = lax.broadcasted_iota(jnp.int32, (tq, tk), 1)
            extra = jnp.where((col // CHUNK) <= (row // CHUNK), 0.0, -jnp.inf)
        keep = None
        if has_sel:
            keep = sel_ref[0, 0, j].astype(jnp.int32) != 0
        for mi in range(n_maps):
            s = _dot_nt(qs[mi], k)
            if extra is not None:
                s = s + extra
            if keep is not None:
                s = jnp.where(keep, s, -jnp.inf)
            m_old = m_ref[mi]
            m_new = jnp.maximum(m_old, jnp.max(s, axis=-1, keepdims=True))
            alpha = jnp.exp(m_old - m_new)
            p = jnp.exp(s - m_new)
            l_ref[mi] = alpha * l_ref[mi] + jnp.sum(p, axis=-1, keepdims=True)
            acc_ref[mi] = alpha * acc_ref[mi] + _dot(p.astype(BF), v)
            m_ref[mi] = m_new

    def far_body(j, carry):
        tile(j, 0)
        return carry

    lax.fori_loop(0, jnp.maximum(j_diag - (n_near - 1), 0), far_body, 0)
    for n in range(1, n_near + 1):
        j = j_diag - n_near + n
        if n == n_near:
            tile(j, n)
        else:
            pl.when(j >= 0)(functools.partial(tile, j, n))

    out = acc_ref[0] * (1.0 / l_ref[0])
    if n_maps == 2:
        lp = lam_ref[...]
        lam = (jnp.exp(jnp.sum(lp[0:1] * lp[1:2], axis=-1, keepdims=True))
               - jnp.exp(jnp.sum(lp[2:3] * lp[3:4], axis=-1, keepdims=True)) + lam_init)
        out = out - lam * (acc_ref[1] * (1.0 / l_ref[1]))
        out = _rms(out, gsub_ref[...], SUBLN_EPS) * (1.0 - lam_init)
    o_ref[0] = out.astype(o_ref.dtype)


def _flash(q, k, v, *, tq, past, dq, bias=None, sel=None, lam_params=None, g_subln=None,
           lam_init=0.0):
    b, sq, _ = q.shape
    sk = k.shape[1]
    tk = tq
    n_maps = 2 if lam_params is not None else 1
    ins = [q, k, v]
    in_specs = [
        pl.BlockSpec((1, tq, dq), lambda bi, h, i: (bi, i, h)),
        pl.BlockSpec((1, sk, dq), lambda bi, h, i: (bi, 0, h)),
        pl.BlockSpec((1, sk, HEAD_W), lambda bi, h, i: (bi, 0, h)),
    ]
    if bias is not None:
        ins.append(bias)
        in_specs.append(pl.BlockSpec((1,) + bias.shape[1:], lambda bi, h, i: (h, 0, 0, 0)))
    if sel is not None:
        ins.append(sel)
        in_specs.append(pl.BlockSpec((1, 1) + sel.shape[2:], lambda bi, h, i: (bi, i, 0, 0, 0)))
    if n_maps == 2:
        ins += [lam_params, g_subln.reshape(1, -1)]
        in_specs += [pl.BlockSpec(lam_params.shape, lambda bi, h, i: (0, 0)),
                     pl.BlockSpec((1, HEAD_W), lambda bi, h, i: (0, 0))]
    kern = functools.partial(_flash_kernel, tq=tq, tk=tk, past=past, n_maps=n_maps,
                             has_bias=bias is not None, has_sel=sel is not None,
                             lam_init=lam_init)
    return pl.pallas_call(
        kern,
        grid=(b, HEADS, sq // tq),
        in_specs=in_specs,
        out_specs=pl.BlockSpec((1, tq, HEAD_W), lambda bi, h, i: (bi, i, h)),
        out_shape=jax.ShapeDtypeStruct((b, sq, HEADS * HEAD_W), BF),
        scratch_shapes=[pltpu.VMEM((n_maps, tq, 1), F32), pltpu.VMEM((n_maps, tq, 1), F32),
                        pltpu.VMEM((n_maps, tq, HEAD_W), F32)],
        compiler_params=_params(("parallel", "parallel", "arbitrary")),
        name="flash",
    )(*ins)


def _post_kernel(*refs, n_parts, final):
    x_ref = refs[0]
    part_refs = refs[1:1 + n_parts]
    wout_refs = refs[1 + n_parts:1 + 2 * n_parts]
    gffn_ref, wg_ref, wu_ref, wd_ref = refs[1 + 2 * n_parts:5 + 2 * n_parts]
    gfin_ref = refs[5 + 2 * n_parts] if final else None
    o_ref = refs[-1]
    x = x_ref[...]
    for p in range(n_parts):
        x = x + _dot(part_refs[p][...], wout_refs[p][...])
    h = _rms(x, gffn_ref[...], NORM_EPS).astype(BF)
    o_ref[...] = x
    for c in range(D_FF // FF_CHUNK):
        sl = slice(c * FF_CHUNK, (c + 1) * FF_CHUNK)
        g = _dot(h, wg_ref[:, sl])
        u = _dot(h, wu_ref[:, sl])
        a = (g * (1.0 / (1.0 + jnp.exp(-g))) * u).astype(BF)
        o_ref[...] += _dot(a, wd_ref[sl, :])
    if final:
        o_ref[...] = _rms(o_ref[...], gfin_ref[...], NORM_EPS)


def _post(x2d, parts, w_out, g_ffn, w_gate, w_up, w_down, g_final, tm):
    t = x2d.shape[0]
    final = g_final is not None
    row = lambda w: pl.BlockSpec((tm, w), lambda i: (i, 0))
    w_out = w_out.astype(BF)
    consts = [w_out[p * 1024:(p + 1) * 1024] for p in range(len(parts))]
    consts += [g_ffn.reshape(1, -1), w_gate.astype(BF), w_up.astype(BF), w_down.astype(BF)]
    if final:
        consts.append(g_final.reshape(1, -1))
    return pl.pallas_call(
        functools.partial(_post_kernel, n_parts=len(parts), final=final),
        grid=(t // tm,),
        in_specs=[row(D_MODEL)] + [row(1024) for _ in parts] + [_const_spec(c.shape) for c in consts],
        out_specs=row(D_MODEL),
        out_shape=jax.ShapeDtypeStruct((t, D_MODEL), F32),
        compiler_params=_params(("parallel",)),
        name="post",
    )(x2d, *parts, *consts)


def _odd_proj_kernel(x_ref, g_ref, wm_ref, wki_ref, ww_ref,
                     q_ref, k_ref, v_ref, k16_ref, v16_ref, qi_ref, ki_ref, ki16_ref, w_out_ref):
    h = _rms(x_ref[...], g_ref[...], NORM_EPS).astype(BF)
    q_ref[...] = (_dot(h, wm_ref[:, 0:1024]) * DSA_SCALE).astype(BF)
    k = _dot(h, wm_ref[:, 1024:2048])
    k_ref[...] = k
    k16_ref[...] = k.astype(BF)
    v = _dot(h, wm_ref[:, 2048:3072])
    v_ref[...] = v
    v16_ref[...] = v.astype(BF)
    qi_ref[...] = _dot(h, wm_ref[:, 3072:3584]).astype(BF)
    ki = _dot(h, wki_ref[...])
    ki_ref[...] = ki
    ki16_ref[...] = ki.astype(BF)
    w_out_ref[...] = _dot(h, ww_ref[...]) * IDX_SCALE


def _odd_proj(x2d, tm, g_attn, w_in):
    t = x2d.shape[0]
    c0 = 3 * 1024 + HEADS * IDX_DIM
    wm = w_in[:, :c0].astype(BF)
    wki = w_in[:, c0:c0 + IDX_DIM].astype(BF)
    ww = jnp.pad(w_in[:, c0 + IDX_DIM:], ((0, 0), (0, LANES - HEADS))).astype(BF)
    row = lambda w: pl.BlockSpec((tm, w), lambda i: (i, 0))
    out_shapes = [
        jax.ShapeDtypeStruct((t, 1024), BF),
        jax.ShapeDtypeStruct((t, 1024), F32),
        jax.ShapeDtypeStruct((t, 1024), F32),
        jax.ShapeDtypeStruct((t, 1024), BF),
        jax.ShapeDtypeStruct((t, 1024), BF),
        jax.ShapeDtypeStruct((t, HEADS * IDX_DIM), BF),
        jax.ShapeDtypeStruct((t, IDX_DIM), F32),
        jax.ShapeDtypeStruct((t, IDX_DIM), BF),
        jax.ShapeDtypeStruct((t, LANES), F32),
    ]
    return pl.pallas_call(
        _odd_proj_kernel,
        grid=(t // tm,),
        in_specs=[row(D_MODEL), _const_spec((1, D_MODEL)), _const_spec(wm.shape),
                  _const_spec(wki.shape), _const_spec(ww.shape)],
        out_specs=[row(s.shape[1]) for s in out_shapes],
        out_shape=out_shapes,
        compiler_params=_params(("parallel",)),
        name="odd_proj",
    )(x2d, g_attn.reshape(1, -1), wm, wki, ww)


def _indexer_kernel(qi_ref, w_ref, k_ref, sel_ref, key_ref, cut_ref, *, tq, tk, past, top_k, nk,
                    idx_bits):
    qblk = pl.program_id(1)
    j_diag = past // tk + qblk
    nkv = j_diag + 1
    q0 = past + qblk * tq
    q = qi_ref[0]
    w = w_ref[0]
    qh = [q[:, h * IDX_DIM:(h + 1) * IDX_DIM] for h in range(HEADS)]
    wh = [w[:, h:h + 1] for h in range(HEADS)]
    row = lax.broadcasted_iota(jnp.int32, (tq, tk), 0)
    col = lax.broadcasted_iota(jnp.int32, (tq, tk), 1)
    row_chunk = (q0 + row) // CHUNK
    kf = float(top_k)

    def score_body(j, carry):
        k = k_ref[0, pl.ds(pl.multiple_of(j * tk, tk), tk), :]
        sc = jnp.zeros((tq, tk), F32)
        for h in range(HEADS):
            sc = sc + wh[h] * jnp.maximum(_dot_nt(qh[h], k), 0.0)
        sc = jnp.where(sc == 0.0, 0.0, sc)
        bits = pltpu.bitcast(sc, jnp.int32)
        key = jnp.where(bits < 0, bits ^ 0x7FFFFFFF, bits)
        ok = ((j * tk + col) // CHUNK <= row_chunk) & (sc > -jnp.inf)
        key_ref[j] = jnp.where(ok, key, INT_MIN)
        return carry

    lax.fori_loop(0, nkv, score_body, 0)

    def count(pred):
        def body(j, acc):
            return acc + jnp.where(pred(key_ref[j], j), 1.0, 0.0)
        acc = lax.fori_loop(0, nkv, body, jnp.zeros((tq, tk), F32))
        return jnp.sum(acc, axis=-1, keepdims=True)

    c0 = count(lambda key, j: key >= 0)
    thr = jnp.where(c0 >= kf, 0, INT_MIN).astype(jnp.int32)

    def bit_body(i, thr):
        cand = thr + lax.shift_left(jnp.int32(1), 30 - i)
        c = count(lambda key, j: key >= cand)
        return jnp.where(c >= kf, cand, thr)

    thr = lax.fori_loop(0, 31, bit_body, thr)

    c_gt = count(lambda key, j: key > thr)
    c_ge = count(lambda key, j: key >= thr)
    need = kf - c_gt
    real = thr > INT_MIN
    excess = real & (c_ge - c_gt > need)
    cut_ref[...] = jnp.full((tq, 1), 2 ** 30, jnp.int32)

    @pl.when(jnp.max(jnp.where(excess, 1.0, 0.0)) > 0.0)
    def _():
        def idx_body(i, cut):
            cand = cut + lax.shift_left(jnp.int32(1), idx_bits - 1 - i)
            c = count(lambda key, j: (key == thr) & (j * tk + col < cand))
            return jnp.where(c < need, cand, cut)
        cut_ref[...] = lax.fori_loop(0, idx_bits, idx_body, jnp.zeros((tq, 1), jnp.int32))

    cut = cut_ref[...]

    def out_body(j, carry):
        key = key_ref[j]
        keep = (key > thr) | ((key == thr) & real & (j * tk + col <= cut))
        sel_ref[0, 0, j] = jnp.where(keep, 1, 0).astype(jnp.int8)
        return carry

    lax.fori_loop(0, nkv, out_body, 0)

    def zero_body(j, carry):
        sel_ref[0, 0, j] = jnp.zeros((tq, tk), jnp.int8)
        return carry

    lax.fori_loop(nkv, nk, zero_body, 0)


def _indexer(qidx, widx, kidx16, *, tq, past, top_k):
    b, sq, _ = qidx.shape
    sk = kidx16.shape[1]
    tk = tq
    nk = sk // tk
    idx_bits = max(1, (sk - 1).bit_length())
    kern = functools.partial(_indexer_kernel, tq=tq, tk=tk, past=past, top_k=top_k, nk=nk,
                             idx_bits=idx_bits)
    return pl.pallas_call(
        kern,
        grid=(b, sq // tq),
        in_specs=[pl.BlockSpec((1, tq, HEADS * IDX_DIM), lambda bi, i: (bi, i, 0)),
                  pl.BlockSpec((1, tq, LANES), lambda bi, i: (bi, i, 0)),
                  pl.BlockSpec((1, sk, IDX_DIM), lambda bi, i: (bi, 0, 0))],
        out_specs=pl.BlockSpec((1, 1, nk, tq, tk), lambda bi, i: (bi, i, 0, 0, 0)),
        out_shape=jax.ShapeDtypeStruct((b, sq // tq, nk, tq, tk), jnp.int8),
        scratch_shapes=[pltpu.VMEM((nk, tq, tk), jnp.int32), pltpu.VMEM((tq, 1), jnp.int32)],
        compiler_params=_params(("parallel", "arbitrary")),
        name="indexer",
    )(qidx, widx, kidx16)


def _rope_tables(pos):
    half = MLA_ROPE // 2
    inv = ROPE_THETA ** (-jnp.arange(half, dtype=F32) / half)
    ang = pos.astype(F32)[:, None] * inv[None, :]
    pad = jnp.zeros((pos.shape[0], HEAD_W - MLA_ROPE), F32)
    cos, sin = jnp.cos(ang), jnp.sin(ang)
    return (jnp.concatenate([cos, cos, pad], axis=1), jnp.concatenate([sin, sin, pad], axis=1))


def _tiles(seq):
    return (256, 256) if seq % 256 == 0 else (CHUNK, CHUNK)


def _even_layer(x, past_len, past, bias, p, lam_init):
    b, s, _ = x.shape
    tm, tq = _tiles(s)
    pos = past_len + jnp.arange(s, dtype=jnp.int32)
    cos_p, sin_p = _rope_tables(pos)
    x2d = x.reshape(b * s, D_MODEL)
    qcat, ckv, kpe, dq, dk, dv, dk16, dv16 = _even_proj(
        x2d, s, tm, p["g_attn"], p["w_in"], p["g_q"], p["w_q_up"], p["g_kv"], cos_p, sin_p)
    r3 = lambda a: a.reshape(b, s, -1)
    if past is None:
        ckv_all, kpe_all, dk_all, dv_all = r3(ckv), r3(kpe), r3(dk16), r3(dv16)
    else:
        c_ckv, c_kpe, c_dk, c_dv = past
        ckv_all = jnp.concatenate([c_ckv, r3(ckv)], axis=1)
        kpe_all = jnp.concatenate([c_kpe, r3(kpe)], axis=1)
        dk_all = jnp.concatenate([c_dk.reshape(b, past_len, -1).astype(BF), r3(dk16)], axis=1)
        dv_all = jnp.concatenate([c_dv.reshape(b, past_len, -1).astype(BF), r3(dv16)], axis=1)
    sk = ckv_all.shape[1]
    kcat, vmla = _kv_up(ckv_all.reshape(b * sk, KV_LORA), kpe_all.reshape(b * sk, MLA_ROPE),
                        p["w_kv_up"], tm)
    a_out = _flash(r3(qcat), kcat.reshape(b, sk, -1), vmla.reshape(b, sk, -1),
                   tq=tq, past=past_len, dq=MLA_QK)
    lam_params = jnp.stack([p["lq1"], p["lk1"], p["lq2"], p["lk2"]]).astype(F32)
    b_out = _flash(r3(dq), dk_all, dv_all, tq=tq, past=past_len, dq=HEAD_W, bias=bias,
                   lam_params=lam_params, g_subln=p["g_subln"], lam_init=lam_init)
    parts = [a_out.reshape(b * s, -1), b_out.reshape(b * s, -1)]
    new = (r3(ckv), r3(kpe), dk.reshape(b, s, HEADS, 2 * DIFF_DIM), dv.reshape(b, s, HEADS, 2 * DIFF_DIM))
    return x2d, parts, new


def _odd_layer(x, past_len, past, bias, p):
    b, s, _ = x.shape
    tm, tq = _tiles(s)
    x2d = x.reshape(b * s, D_MODEL)
    q, k, v, k16, v16, qi, ki, ki16, widx = _odd_proj(x2d, tm, p["g_attn"], p["w_in"])
    r3 = lambda a: a.reshape(b, s, -1)
    if past is None:
        k_all, v_all, ki_all = r3(k16), r3(v16), r3(ki16)
    else:
        c_k, c_v, c_ki = past
        k_all = jnp.concatenate([c_k.reshape(b, past_len, -1).astype(BF), r3(k16)], axis=1)
        v_all = jnp.concatenate([c_v.reshape(b, past_len, -1).astype(BF), r3(v16)], axis=1)
        ki_all = jnp.concatenate([c_ki.astype(BF), r3(ki16)], axis=1)
    sk = k_all.shape[1]
    top_k = min(DSA_TOPK, sk // 4)
    sel = _indexer(r3(qi), r3(widx), ki_all, tq=tq, past=past_len, top_k=top_k)
    out = _flash(r3(q), k_all, v_all, tq=tq, past=past_len, dq=HEAD_W, bias=bias, sel=sel)
    new = (k.reshape(b, s, HEADS, DSA_DIM), v.reshape(b, s, HEADS, DSA_DIM), r3(ki))
    return x2d, [out.reshape(b * s, -1)], new


def kernel(x_prompt, x_sample, cache_mla_ckv, cache_mla_kpe, cache_diff_k, cache_diff_v, cache_dsa_k, cache_dsa_v, cache_dsa_kidx, g_attn_even, w_in_even, g_q_lora, w_q_up, g_kv_lora, w_kv_up, lambda_q1, lambda_k1, lambda_q2, lambda_k2, g_diff_subln, w_out_even, g_attn_odd, w_in_odd, w_out_odd, rel_bias_table, g_ffn, w_gate, w_up, w_down, g_final):
    depth = g_ffn.shape[0]
    past_len = cache_mla_ckv.shape[2]
    groups = [
        dict(x=x_prompt, past_len=0, has_past=False),
        dict(x=x_sample, past_len=past_len, has_past=True),
    ]
    results = []
    for grp in groups:
        x = grp["x"]
        b, s, _ = x.shape
        tm, tq = _tiles(s)
        bias = _bias_tiles(rel_bias_table, tq, tq)
        even_new, odd_new = [], []
        for layer in range(depth):
            if layer % 2 == 0:
                e = layer // 2
                lam_init = 0.8 - 0.6 * math.exp(-0.3 * layer)
                p = dict(g_attn=g_attn_even[e], w_in=w_in_even[e], g_q=g_q_lora[e], w_q_up=w_q_up[e],
                         g_kv=g_kv_lora[e], w_kv_up=w_kv_up[e], lq1=lambda_q1[e], lk1=lambda_k1[e],
                         lq2=lambda_q2[e], lk2=lambda_k2[e], g_subln=g_diff_subln[e])
                past = ((cache_mla_ckv[e], cache_mla_kpe[e], cache_diff_k[e], cache_diff_v[e])
                        if grp["has_past"] else None)
                x2d, parts, new = _even_layer(x, grp["past_len"], past, bias, p, lam_init)
                even_new.append(new)
                w_out = w_out_even[e]
            else:
                o = layer // 2
                p = dict(g_attn=g_attn_odd[o], w_in=w_in_odd[o])
                past = ((cache_dsa_k[o], cache_dsa_v[o], cache_dsa_kidx[o])
                        if grp["has_past"] else None)
                x2d, parts, new = _odd_layer(x, grp["past_len"], past, bias, p)
                odd_new.append(new)
                w_out = w_out_odd[o]
            gfin = g_final if layer == depth - 1 else None
            x = _post(x2d, parts, w_out, g_ffn[layer], w_gate[layer], w_up[layer], w_down[layer],
                      gfin, tm).reshape(b, s, D_MODEL)
        results.append((x, [jnp.stack(a) for a in zip(*even_new)], [jnp.stack(a) for a in zip(*odd_new)]))
    (yp, ep, op), (ys, es, os_) = results
    return (yp, ys, *ep, *op, *es, *os_)
```

```python
import functools
import math

import jax
import jax.numpy as jnp
from jax import lax
from jax.experimental import pallas as pl
from jax.experimental.pallas import tpu as pltpu

D_MODEL = 1024
CHUNK = 64
NORM_EPS = 1e-6
SUBLN_EPS = 1e-5
HEADS = 8
MLA_NOPE = 128
MLA_ROPE = 64
MLA_V = 128
Q_LORA = 384
KV_LORA = 256
ROPE_THETA = 10000.0
LOG2E = math.log2(math.e)
MLA_SCALE = (MLA_NOPE + MLA_ROPE) ** -0.5 * LOG2E
DIFF_DIM = 64
DIFF_SCALE = DIFF_DIM ** -0.5 * LOG2E
DSA_DIM = 128
DSA_SCALE = DSA_DIM ** -0.5 * LOG2E
IDX_DIM = 64
IDX_SCALE = IDX_DIM ** -0.5 * HEADS ** -0.5
DSA_TOPK = 256
NUM_BUCKETS = 32
T5_FAR = 91
HEAD_W = 128
MLA_QK = 256
D_FF = -(-8 * D_MODEL // (3 * 256)) * 256
FF_CHUNK = 256
HEADS_PER_STEP = 8

LANES = 128
VMEM_LIMIT = 56 * 1024 * 1024

INT_MIN = -2 ** 31
NEG_BIG = -1e30
BF = jnp.bfloat16
F32 = jnp.float32


def _dot(a, b):
    return jnp.dot(a, b, preferred_element_type=F32)


def _dot_nt(a, b):
    return lax.dot_general(a, b, (((1,), (1,)), ((), ())), preferred_element_type=F32)


def _rms(x, g, eps):
    return x * lax.rsqrt(jnp.mean(x * x, axis=-1, keepdims=True) + eps) * g


def _params(sem):
    return pltpu.CompilerParams(dimension_semantics=sem, vmem_limit_bytes=VMEM_LIMIT)


def _const_spec(shape):
    nd = len(shape)
    return pl.BlockSpec(shape, lambda *_: (0,) * nd, pipeline_mode=pl.Buffered(1))


def _even_proj_kernel(x_ref, g_ref, wq_ref, wkv_ref, wd_ref, wkpe_ref, wkrot_ref, gq_ref,
                      wqcat_ref, wqrot_ref, gkv_ref, cos_ref, sin_ref,
                      qcat_ref, ckv_ref, kpe_ref, dq_ref, dk_ref, dv_ref, dk16_ref, dv16_ref):
    h = _rms(x_ref[...], g_ref[...], NORM_EPS).astype(BF)
    cos = cos_ref[...]
    sin = sin_ref[...]
    qn = _rms(_dot(h, wq_ref[...]), gq_ref[...], NORM_EPS).astype(BF)
    qc = _dot(qn, wqcat_ref[...])
    qr = _dot(qn, wqrot_ref[...])
    for hh in range(HEADS):
        a = hh * MLA_QK
        qcat_ref[:, a:a + HEAD_W] = (qc[:, a:a + HEAD_W] * MLA_SCALE).astype(BF)
        pe = qc[:, a + HEAD_W:a + MLA_QK] * cos + qr[:, hh * HEAD_W:(hh + 1) * HEAD_W] * sin
        qcat_ref[:, a + HEAD_W:a + MLA_QK] = (pe * MLA_SCALE).astype(BF)
    ckv_ref[...] = _rms(_dot(h, wkv_ref[...]), gkv_ref[...], NORM_EPS)
    kpe_ref[...] = (_dot(h, wkpe_ref[...]) * cos[:, :MLA_ROPE]
                    + _dot(h, wkrot_ref[...]) * sin[:, :MLA_ROPE])
    dq_ref[...] = (_dot(h, wd_ref[:, 0:1024]) * DIFF_SCALE).astype(BF)
    dk = _dot(h, wd_ref[:, 1024:2048])
    dk_ref[...] = dk
    dk16_ref[...] = dk.astype(BF)
    dv = _dot(h, wd_ref[:, 2048:3072])
    dv_ref[...] = dv
    dv16_ref[...] = dv.astype(BF)


def _even_proj(x2d, seq, tm, g_attn, w_in, g_q, w_q_up, g_kv, cos_p, sin_p):
    t = x2d.shape[0]
    c0, c1, c2 = Q_LORA, Q_LORA + KV_LORA, Q_LORA + KV_LORA + MLA_ROPE
    half = MLA_ROPE // 2
    wq = w_in[:, :c0].astype(BF)
    wkv = w_in[:, c0:c1].astype(BF)
    wkpe = w_in[:, c1:c2]
    wkrot = jnp.concatenate([-wkpe[:, half:], wkpe[:, :half]], axis=1).astype(BF)
    wkpe = wkpe.astype(BF)
    wd = w_in[:, c2:].astype(BF)
    wqu = w_q_up.reshape(Q_LORA, HEADS, MLA_NOPE + MLA_ROPE)
    w_nope, w_pe = wqu[..., :MLA_NOPE], wqu[..., MLA_NOPE:]
    w_rot = jnp.concatenate([-w_pe[..., half:], w_pe[..., :half]], axis=-1)
    zpad = jnp.zeros((Q_LORA, HEADS, MLA_QK - MLA_NOPE - MLA_ROPE), F32)
    wqcat = jnp.concatenate([w_nope, w_pe, zpad], axis=-1).reshape(Q_LORA, HEADS * MLA_QK).astype(BF)
    wqrot = jnp.concatenate([w_rot, zpad], axis=-1).reshape(Q_LORA, HEADS * HEAD_W).astype(BF)
    n_pos = seq // tm
    row = lambda w: pl.BlockSpec((tm, w), lambda i: (i, 0))
    pos = pl.BlockSpec((tm, HEAD_W), lambda i: (i % n_pos, 0))
    out_shapes = [
        jax.ShapeDtypeStruct((t, HEADS * MLA_QK), BF),
        jax.ShapeDtypeStruct((t, KV_LORA), F32),
        jax.ShapeDtypeStruct((t, MLA_ROPE), F32),
        jax.ShapeDtypeStruct((t, 1024), BF),
        jax.ShapeDtypeStruct((t, 1024), F32),
        jax.ShapeDtypeStruct((t, 1024), F32),
        jax.ShapeDtypeStruct((t, 1024), BF),
        jax.ShapeDtypeStruct((t, 1024), BF),
    ]
    ins = [x2d, g_attn.reshape(1, -1), wq, wkv, wd, wkpe, wkrot, g_q.reshape(1, -1), wqcat, wqrot,
           g_kv.reshape(1, -1), cos_p, sin_p]
    in_specs = [row(D_MODEL)] + [_const_spec(a.shape) for a in ins[1:11]] + [pos, pos]
    return pl.pallas_call(
        _even_proj_kernel,
        grid=(t // tm,),
        in_specs=in_specs,
        out_specs=[row(s.shape[1]) for s in out_shapes],
        out_shape=out_shapes,
        compiler_params=_params(("parallel",)),
        name="even_proj",
    )(*ins)


def _kv_up_kernel(ckv_ref, kpe_ref, wk_ref, wv_ref, kcat_ref, v_ref):
    c = ckv_ref[...].astype(BF)
    kn = _dot(c, wk_ref[...])
    v_ref[...] = _dot(c, wv_ref[...]).astype(BF)
    kpe = kpe_ref[...]
    kp = jnp.concatenate([kpe, jnp.zeros_like(kpe)], axis=-1).astype(BF)
    for hh in range(HEADS):
        a = hh * MLA_QK
        kcat_ref[:, a:a + HEAD_W] = kn[:, hh * HEAD_W:(hh + 1) * HEAD_W].astype(BF)
        kcat_ref[:, a + HEAD_W:a + MLA_QK] = kp


def _kv_up(ckv2d, kpe2d, w_kv_up, tm):
    t = ckv2d.shape[0]
    wkv = w_kv_up.reshape(KV_LORA, HEADS, MLA_NOPE + MLA_V)
    wk = wkv[..., :MLA_NOPE].reshape(KV_LORA, HEADS * MLA_NOPE).astype(BF)
    wv = wkv[..., MLA_NOPE:].reshape(KV_LORA, HEADS * MLA_V).astype(BF)
    row = lambda w: pl.BlockSpec((tm, w), lambda i: (i, 0))
    return pl.pallas_call(
        _kv_up_kernel,
        grid=(t // tm,),
        in_specs=[row(KV_LORA), row(MLA_ROPE), _const_spec(wk.shape), _const_spec(wv.shape)],
        out_specs=[row(HEADS * MLA_QK), row(HEADS * MLA_V)],
        out_shape=[jax.ShapeDtypeStruct((t, HEADS * MLA_QK), BF),
                   jax.ShapeDtypeStruct((t, HEADS * MLA_V), BF)],
        compiler_params=_params(("parallel",)),
        name="kv_up",
    )(ckv2d, kpe2d, wk, wv)


def _n_near(tk):
    return 1 + -(-(T5_FAR - 1) // tk)


def _bias_kernel(tab_ref, out_ref, *, tq, tk, n_near):
    h = pl.program_id(0)
    row = lax.broadcasted_iota(jnp.int32, (tq, tk), 0)
    col = lax.broadcasted_iota(jnp.int32, (tq, tk), 1)
    far = tab_ref[NUM_BUCKETS // 2 - 1, h]
    for n in range(1, n_near + 1):
        rel = col - row - (n_near - n) * tk
        dist = jnp.abs(rel)
        large = jnp.full((tq, tk), NUM_BUCKETS // 4, jnp.int32)
        for thr in (12, 16, 23, 32, 46, 64, T5_FAR):
            large = large + jnp.where(dist >= thr, 1, 0)
        bucket = jnp.where(rel > 0, NUM_BUCKETS // 2, 0) + jnp.where(dist < NUM_BUCKETS // 4, dist, large)
        val = jnp.zeros((tq, tk), F32)
        for b in range(NUM_BUCKETS):
            val = jnp.where(bucket == b, tab_ref[b, h], val)
        val = (val - far) * LOG2E
        if n == n_near:
            val = jnp.where((col // CHUNK) <= (row // CHUNK), val, -jnp.inf)
        out_ref[0, n - 1] = val


def _bias_tiles(rel_table, tq, tk):
    n_near = _n_near(tk)
    return pl.pallas_call(
        functools.partial(_bias_kernel, tq=tq, tk=tk, n_near=n_near),
        grid=(HEADS,),
        in_specs=[pl.BlockSpec(memory_space=pltpu.SMEM)],
        out_specs=pl.BlockSpec((1, n_near, tq, tk), lambda h: (h, 0, 0, 0)),
        out_shape=jax.ShapeDtypeStruct((HEADS, n_near, tq, tk), F32),
        compiler_params=_params(("arbitrary",)),
        name="bias_tiles",
    )(rel_table)


def _flash_kernel(*refs, tq, tk, past, n_maps, has_bias, has_sel, lam_init, hg, dq):
    refs = list(refs)
    q_ref, k_ref, v_ref = refs[:3]
    pos = 3
    bias_ref = sel_ref = lam_ref = gsub_ref = None
    if has_bias:
        bias_ref = refs[pos]; pos += 1
    if has_sel:
        sel_ref = refs[pos]; pos += 1
    if n_maps == 2:
        lam_ref, gsub_ref = refs[pos], refs[pos + 1]; pos += 2
    o_ref, m_ref, l_ref, acc_ref = refs[pos:pos + 4]
    n_near = _n_near(tk) if has_bias else 1
    sw = m_ref.shape[-1]
    n_col = tk // sw if sw > 1 else 1

    j_diag = past // tk + pl.program_id(2)
    qs = []
    for h in range(hg):
        q = q_ref[0, :, h * dq:(h + 1) * dq]
        if n_maps == 2:
            lane = lax.broadcasted_iota(jnp.int32, q.shape, 1)
            qs.append([jnp.where(lane < DIFF_DIM, q, jnp.zeros_like(q)),
                       jnp.where(lane >= DIFF_DIM, q, jnp.zeros_like(q))])
        else:
            qs.append([q])
    m_ref[...] = jnp.full(m_ref.shape, NEG_BIG, F32)
    l_ref[...] = jnp.zeros(l_ref.shape, F32)
    acc_ref[...] = jnp.zeros(acc_ref.shape, F32)

    def tile(j, near):
        start = pl.multiple_of(j * tk, tk)
        causal = None
        if not has_bias and near is not None:
            row = lax.broadcasted_iota(jnp.int32, (tq, tk), 0)
            col = lax.broadcasted_iota(jnp.int32, (tq, tk), 1)
            causal = jnp.where((col // CHUNK) <= (row // CHUNK), 0.0, -jnp.inf)
        keep = None
        if has_sel:
            keep = sel_ref[0, 0, j].astype(jnp.int32) != 0
        for h in range(hg):
            k = k_ref[0, pl.ds(start, tk), h * dq:(h + 1) * dq]
            v = v_ref[0, pl.ds(start, tk), h * HEAD_W:(h + 1) * HEAD_W]
            extra = causal
            if has_bias and near is not None:
                extra = bias_ref[h, near]
            for mi in range(n_maps):
                i = h * n_maps + mi
                s = _dot_nt(qs[h][mi], k)
                if extra is not None:
                    s = s + extra
                if keep is not None:
                    s = jnp.where(keep, s, -jnp.inf)
                if sw > 1:
                    cols = [s[:, c * sw:(c + 1) * sw] for c in range(n_col)]
                else:
                    cols = [s]
                mx = cols[0]
                for c in cols[1:]:
                    mx = jnp.maximum(mx, c)
                m_old = m_ref[i]
                m_new = jnp.maximum(m_old, jnp.max(mx, axis=-1, keepdims=True))
                alpha = jnp.exp2(m_old - m_new)
                ps = [jnp.exp2(c - m_new) for c in cols]
                psum = ps[0]
                for c in ps[1:]:
                    psum = psum + c
                l_ref[i] = alpha * l_ref[i] + jnp.sum(psum, axis=-1, keepdims=True)
                p = ps[0] if len(ps) == 1 else jnp.concatenate(ps, axis=-1)
                acc_ref[i] = alpha * acc_ref[i] + _dot(p.astype(BF), v)
                m_ref[i] = m_new

    def far_body(j, carry):
        tile(j, None)
        return carry

    lax.fori_loop(0, jnp.maximum(j_diag - (n_near - 1), 0), far_body, 0)
    for n in range(n_near):
        j = j_diag - (n_near - 1) + n
        if n == n_near - 1:
            tile(j, n)
        else:
            pl.when(j >= 0)(functools.partial(tile, j, n))

    if n_maps == 2:
        lp = lam_ref[...]
        lam = (jnp.exp(jnp.sum(lp[0:1] * lp[1:2], axis=-1, keepdims=True))
               - jnp.exp(jnp.sum(lp[2:3] * lp[3:4], axis=-1, keepdims=True)) + lam_init)
    for h in range(hg):
        i = h * n_maps
        out = acc_ref[i] * (1.0 / l_ref[i])
        if n_maps == 2:
            out = out - lam * (acc_ref[i + 1] * (1.0 / l_ref[i + 1]))
            out = _rms(out, gsub_ref[...], SUBLN_EPS) * (1.0 - lam_init)
        o_ref[0, :, h * HEAD_W:(h + 1) * HEAD_W] = out.astype(o_ref.dtype)


def _flash(q, k, v, *, tq, past, dq, bias=None, sel=None, lam_params=None, g_subln=None,
           lam_init=0.0):
    b, sq, _ = q.shape
    sk = k.shape[1]
    tk = tq
    n_maps = 2 if lam_params is not None else 1
    hg = HEADS_PER_STEP
    sw = LANES if tk % LANES == 0 else 1
    ins = [q, k, v]
    once = pl.Buffered(1)
    in_specs = [
        pl.BlockSpec((1, tq, hg * dq), lambda bi, h, i: (bi, i, h)),
        pl.BlockSpec((1, sk, hg * dq), lambda bi, h, i: (bi, 0, h), pipeline_mode=once),
        pl.BlockSpec((1, sk, hg * HEAD_W), lambda bi, h, i: (bi, 0, h), pipeline_mode=once),
    ]
    if bias is not None:
        ins.append(bias)
        in_specs.append(pl.BlockSpec((hg,) + bias.shape[1:], lambda bi, h, i: (h, 0, 0, 0),
                                     pipeline_mode=once))
    if sel is not None:
        ins.append(sel)
        in_specs.append(pl.BlockSpec((1, 1) + sel.shape[2:], lambda bi, h, i: (bi, i, 0, 0, 0)))
    if n_maps == 2:
        ins += [lam_params, g_subln.reshape(1, -1)]
        in_specs += [pl.BlockSpec(lam_params.shape, lambda bi, h, i: (0, 0)),
                     pl.BlockSpec((1, HEAD_W), lambda bi, h, i: (0, 0))]
    kern = functools.partial(_flash_kernel, tq=tq, tk=tk, past=past, n_maps=n_maps,
                             has_bias=bias is not None, has_sel=sel is not None,
                             lam_init=lam_init, hg=hg, dq=dq)
    return pl.pallas_call(
        kern,
        grid=(b, HEADS // hg, sq // tq),
        in_specs=in_specs,
        out_specs=pl.BlockSpec((1, tq, hg * HEAD_W), lambda bi, h, i: (bi, i, h)),
        out_shape=jax.ShapeDtypeStruct((b, sq, HEADS * HEAD_W), BF),
        scratch_shapes=[pltpu.VMEM((hg * n_maps, tq, sw), F32), pltpu.VMEM((hg * n_maps, tq, sw), F32),
                        pltpu.VMEM((hg * n_maps, tq, HEAD_W), F32)],
        compiler_params=_params(("parallel", "parallel", "arbitrary")),
        name="flash",
    )(*ins)


def _post_kernel(*refs, n_parts, final):
    x_ref = refs[0]
    part_refs = refs[1:1 + n_parts]
    wout_refs = refs[1 + n_parts:1 + 2 * n_parts]
    gffn_ref, wg_ref, wu_ref, wd_ref = refs[1 + 2 * n_parts:5 + 2 * n_parts]
    gfin_ref = refs[5 + 2 * n_parts] if final else None
    o_ref = refs[-1]
    x = x_ref[...]
    for p in range(n_parts):
        x = x + _dot(part_refs[p][...], wout_refs[p][...])
    h = _rms(x, gffn_ref[...], NORM_EPS).astype(BF)
    o_ref[...] = x
    for c in range(D_FF // FF_CHUNK):
        sl = slice(c * FF_CHUNK, (c + 1) * FF_CHUNK)
        g = _dot(h, wg_ref[:, sl])
        u = _dot(h, wu_ref[:, sl])
        a = (g * (1.0 / (1.0 + jnp.exp(-g))) * u).astype(BF)
        o_ref[...] += _dot(a, wd_ref[sl, :])
    if final:
        o_ref[...] = _rms(o_ref[...], gfin_ref[...], NORM_EPS)


def _post(x2d, parts, w_out, g_ffn, w_gate, w_up, w_down, g_final, tm):
    t = x2d.shape[0]
    final = g_final is not None
    row = lambda w: pl.BlockSpec((tm, w), lambda i: (i, 0))
    w_out = w_out.astype(BF)
    consts = [w_out[p * 1024:(p + 1) * 1024] for p in range(len(parts))]
    consts += [g_ffn.reshape(1, -1), w_gate.astype(BF), w_up.astype(BF), w_down.astype(BF)]
    if final:
        consts.append(g_final.reshape(1, -1))
    return pl.pallas_call(
        functools.partial(_post_kernel, n_parts=len(parts), final=final),
        grid=(t // tm,),
        in_specs=[row(D_MODEL)] + [row(1024) for _ in parts] + [_const_spec(c.shape) for c in consts],
        out_specs=row(D_MODEL),
        out_shape=jax.ShapeDtypeStruct((t, D_MODEL), F32),
        compiler_params=_params(("parallel",)),
        name="post",
    )(x2d, *parts, *consts)


def _odd_proj_kernel(x_ref, g_ref, wm_ref, wki_ref, ww_ref,
                     q_ref, k_ref, v_ref, k16_ref, v16_ref, qi_ref, ki_ref, ki16_ref, w_out_ref):
    h = _rms(x_ref[...], g_ref[...], NORM_EPS).astype(BF)
    q_ref[...] = (_dot(h, wm_ref[:, 0:1024]) * DSA_SCALE).astype(BF)
    k = _dot(h, wm_ref[:, 1024:2048])
    k_ref[...] = k
    k16_ref[...] = k.astype(BF)
    v = _dot(h, wm_ref[:, 2048:3072])
    v_ref[...] = v
    v16_ref[...] = v.astype(BF)
    qi_ref[...] = _dot(h, wm_ref[:, 3072:3584]).astype(BF)
    ki = _dot(h, wki_ref[...])
    ki_ref[...] = ki
    ki16_ref[...] = ki.astype(BF)
    w_out_ref[...] = _dot(h, ww_ref[...]) * IDX_SCALE


def _odd_proj(x2d, tm, g_attn, w_in):
    t = x2d.shape[0]
    c0 = 3 * 1024 + HEADS * IDX_DIM
    wm = w_in[:, :c0].astype(BF)
    wki = w_in[:, c0:c0 + IDX_DIM].astype(BF)
    ww = jnp.pad(w_in[:, c0 + IDX_DIM:], ((0, 0), (0, LANES - HEADS))).astype(BF)
    row = lambda w: pl.BlockSpec((tm, w), lambda i: (i, 0))
    out_shapes = [
        jax.ShapeDtypeStruct((t, 1024), BF),
        jax.ShapeDtypeStruct((t, 1024), F32),
        jax.ShapeDtypeStruct((t, 1024), F32),
        jax.ShapeDtypeStruct((t, 1024), BF),
        jax.ShapeDtypeStruct((t, 1024), BF),
        jax.ShapeDtypeStruct((t, HEADS * IDX_DIM), BF),
        jax.ShapeDtypeStruct((t, IDX_DIM), F32),
        jax.ShapeDtypeStruct((t, IDX_DIM), BF),
        jax.ShapeDtypeStruct((t, LANES), F32),
    ]
    return pl.pallas_call(
        _odd_proj_kernel,
        grid=(t // tm,),
        in_specs=[row(D_MODEL), _const_spec((1, D_MODEL)), _const_spec(wm.shape),
                  _const_spec(wki.shape), _const_spec(ww.shape)],
        out_specs=[row(s.shape[1]) for s in out_shapes],
        out_shape=out_shapes,
        compiler_params=_params(("parallel",)),
        name="odd_proj",
    )(x2d, g_attn.reshape(1, -1), wm, wki, ww)


def _indexer_kernel(qi_ref, w_ref, k_ref, sel_ref, key_ref, cut_ref, *, tq, tk, past, top_k, nk,
                    idx_bits):
    qblk = pl.program_id(1)
    j_diag = past // tk + qblk
    nkv = j_diag + 1
    q0 = past + qblk * tq
    q = qi_ref[0]
    w = w_ref[0]
    qh = [q[:, h * IDX_DIM:(h + 1) * IDX_DIM] for h in range(HEADS)]
    wh = [w[:, h:h + 1] for h in range(HEADS)]
    row = lax.broadcasted_iota(jnp.int32, (tq, tk), 0)
    col = lax.broadcasted_iota(jnp.int32, (tq, tk), 1)
    row_chunk = (q0 + row) // CHUNK
    kf = float(top_k)

    def score_body(j, carry):
        k = k_ref[0, pl.ds(pl.multiple_of(j * tk, tk), tk), :]
        sc = jnp.zeros((tq, tk), F32)
        for h in range(HEADS):
            sc = sc + wh[h] * jnp.maximum(_dot_nt(qh[h], k), 0.0)
        sc = jnp.where(sc == 0.0, 0.0, sc)
        bits = pltpu.bitcast(sc, jnp.int32)
        key = jnp.where(bits < 0, bits ^ 0x7FFFFFFF, bits)
        ok = ((j * tk + col) // CHUNK <= row_chunk) & (sc > -jnp.inf)
        key_ref[j] = jnp.where(ok, key, INT_MIN)
        return carry

    lax.fori_loop(0, nkv, score_body, 0)

    def count(pred):
        def body(j, acc):
            return acc + jnp.where(pred(key_ref[j], j), 1.0, 0.0)
        acc = lax.fori_loop(0, nkv, body, jnp.zeros((tq, tk), F32))
        return jnp.sum(acc, axis=-1, keepdims=True)

    c0 = count(lambda key, j: key >= 0)
    thr = jnp.where(c0 >= kf, 0, INT_MIN).astype(jnp.int32)

    def bit_body(i, thr):
        cand = thr + lax.shift_left(jnp.int32(1), 30 - i)
        c = count(lambda key, j: key >= cand)
        return jnp.where(c >= kf, cand, thr)

    thr = lax.fori_loop(0, 31, bit_body, thr)

    c_gt = count(lambda key, j: key > thr)
    c_ge = count(lambda key, j: key >= thr)
    need = kf - c_gt
    real = thr > INT_MIN
    excess = real & (c_ge - c_gt > need)
    cut_ref[...] = jnp.full((tq, 1), 2 ** 30, jnp.int32)

    @pl.when(jnp.max(jnp.where(excess, 1.0, 0.0)) > 0.0)
    def _():
        def idx_body(i, cut):
            cand = cut + lax.shift_left(jnp.int32(1), idx_bits - 1 - i)
            c = count(lambda key, j: (key == thr) & (j * tk + col < cand))
            return jnp.where(c < need, cand, cut)
        cut_ref[...] = lax.fori_loop(0, idx_bits, idx_body, jnp.zeros((tq, 1), jnp.int32))

    cut = cut_ref[...]

    def out_body(j, carry):
        key = key_ref[j]
        keep = (key > thr) | ((key == thr) & real & (j * tk + col <= cut))
        sel_ref[0, 0, j] = jnp.where(keep, 1, 0).astype(jnp.int8)
        return carry

    lax.fori_loop(0, nkv, out_body, 0)

    def zero_body(j, carry):
        sel_ref[0, 0, j] = jnp.zeros((tq, tk), jnp.int8)
        return carry

    lax.fori_loop(nkv, nk, zero_body, 0)


def _indexer(qidx, widx, kidx16, *, tq, past, top_k):
    b, sq, _ = qidx.shape
    sk = kidx16.shape[1]
    tk = tq
    nk = sk // tk
    idx_bits = max(1, (sk - 1).bit_length())
    kern = functools.partial(_indexer_kernel, tq=tq, tk=tk, past=past, top_k=top_k, nk=nk,
                             idx_bits=idx_bits)
    return pl.pallas_call(
        kern,
        grid=(b, sq // tq),
        in_specs=[pl.BlockSpec((1, tq, HEADS * IDX_DIM), lambda bi, i: (bi, i, 0)),
                  pl.BlockSpec((1, tq, LANES), lambda bi, i: (bi, i, 0)),
                  pl.BlockSpec((1, sk, IDX_DIM), lambda bi, i: (bi, 0, 0))],
        out_specs=pl.BlockSpec((1, 1, nk, tq, tk), lambda bi, i: (bi, i, 0, 0, 0)),
        out_shape=jax.ShapeDtypeStruct((b, sq // tq, nk, tq, tk), jnp.int8),
        scratch_shapes=[pltpu.VMEM((nk, tq, tk), jnp.int32), pltpu.VMEM((tq, 1), jnp.int32)],
        compiler_params=_params(("parallel", "arbitrary")),
        name="indexer",
    )(qidx, widx, kidx16)


def _rope_tables(pos):
    half = MLA_ROPE // 2
    inv = ROPE_THETA ** (-jnp.arange(half, dtype=F32) / half)
    ang = pos.astype(F32)[:, None] * inv[None, :]
    pad = jnp.zeros((pos.shape[0], HEAD_W - MLA_ROPE), F32)
    cos, sin = jnp.cos(ang), jnp.sin(ang)
    return (jnp.concatenate([cos, cos, pad], axis=1), jnp.concatenate([sin, sin, pad], axis=1))


def _tiles(seq):
    return (256, 256) if seq % 256 == 0 else (CHUNK, CHUNK)


def _even_layer(x, past_len, past, bias, p, lam_init):
    b, s, _ = x.shape
    tm, tq = _tiles(s)
    pos = past_len + jnp.arange(s, dtype=jnp.int32)
    cos_p, sin_p = _rope_tables(pos)
    x2d = x.reshape(b * s, D_MODEL)
    qcat, ckv, kpe, dq, dk, dv, dk16, dv16 = _even_proj(
        x2d, s, tm, p["g_attn"], p["w_in"], p["g_q"], p["w_q_up"], p["g_kv"], cos_p, sin_p)
    r3 = lambda a: a.reshape(b, s, -1)
    if past is None:
        ckv_all, kpe_all, dk_all, dv_all = r3(ckv), r3(kpe), r3(dk16), r3(dv16)
    else:
        c_ckv, c_kpe, c_dk, c_dv = past
        ckv_all = jnp.concatenate([c_ckv, r3(ckv)], axis=1)
        kpe_all = jnp.concatenate([c_kpe, r3(kpe)], axis=1)
        dk_all = jnp.concatenate([c_dk.reshape(b, past_len, -1).astype(BF), r3(dk16)], axis=1)
        dv_all = jnp.concatenate([c_dv.reshape(b, past_len, -1).astype(BF), r3(dv16)], axis=1)
    sk = ckv_all.shape[1]
    kcat, vmla = _kv_up(ckv_all.reshape(b * sk, KV_LORA), kpe_all.reshape(b * sk, MLA_ROPE),
                        p["w_kv_up"], tm)
    a_out = _flash(r3(qcat), kcat.reshape(b, sk, -1), vmla.reshape(b, sk, -1),
                   tq=tq, past=past_len, dq=MLA_QK)
    lam_params = jnp.stack([p["lq1"], p["lk1"], p["lq2"], p["lk2"]]).astype(F32)
    b_out = _flash(r3(dq), dk_all, dv_all, tq=tq, past=past_len, dq=HEAD_W, bias=bias,
                   lam_params=lam_params, g_subln=p["g_subln"], lam_init=lam_init)
    parts = [a_out.reshape(b * s, -1), b_out.reshape(b * s, -1)]
    new = (r3(ckv), r3(kpe), dk.reshape(b, s, HEADS, 2 * DIFF_DIM), dv.reshape(b, s, HEADS, 2 * DIFF_DIM))
    return x2d, parts, new


def _odd_layer(x, past_len, past, bias, p):
    b, s, _ = x.shape
    tm, tq = _tiles(s)
    x2d = x.reshape(b * s, D_MODEL)
    q, k, v, k16, v16, qi, ki, ki16, widx = _odd_proj(x2d, tm, p["g_attn"], p["w_in"])
    r3 = lambda a: a.reshape(b, s, -1)
    if past is None:
        k_all, v_all, ki_all = r3(k16), r3(v16), r3(ki16)
    else:
        c_k, c_v, c_ki = past
        k_all = jnp.concatenate([c_k.reshape(b, past_len, -1).astype(BF), r3(k16)], axis=1)
        v_all = jnp.concatenate([c_v.reshape(b, past_len, -1).astype(BF), r3(v16)], axis=1)
        ki_all = jnp.concatenate([c_ki.astype(BF), r3(ki16)], axis=1)
    sk = k_all.shape[1]
    top_k = min(DSA_TOPK, sk // 4)
    sel = _indexer(r3(qi), r3(widx), ki_all, tq=tq, past=past_len, top_k=top_k)
    out = _flash(r3(q), k_all, v_all, tq=tq, past=past_len, dq=HEAD_W, bias=bias, sel=sel)
    new = (k.reshape(b, s, HEADS, DSA_DIM), v.reshape(b, s, HEADS, DSA_DIM), r3(ki))
    return x2d, [out.reshape(b * s, -1)], new


def kernel(x_prompt, x_sample, cache_mla_ckv, cache_mla_kpe, cache_diff_k, cache_diff_v, cache_dsa_k, cache_dsa_v, cache_dsa_kidx, g_attn_even, w_in_even, g_q_lora, w_q_up, g_kv_lora, w_kv_up, lambda_q1, lambda_k1, lambda_q2, lambda_k2, g_diff_subln, w_out_even, g_attn_odd, w_in_odd, w_out_odd, rel_bias_table, g_ffn, w_gate, w_up, w_down, g_final):
    depth = g_ffn.shape[0]
    past_len = cache_mla_ckv.shape[2]
    groups = [
        dict(x=x_prompt, past_len=0, has_past=False),
        dict(x=x_sample, past_len=past_len, has_past=True),
    ]
    results = []
    for grp in groups:
        x = grp["x"]
        b, s, _ = x.shape
        tm, tq = _tiles(s)
        bias = _bias_tiles(rel_bias_table, tq, tq)
        even_new, odd_new = [], []
        for layer in range(depth):
            if layer % 2 == 0:
                e = layer // 2
                lam_init = 0.8 - 0.6 * math.exp(-0.3 * layer)
                p = dict(g_attn=g_attn_even[e], w_in=w_in_even[e], g_q=g_q_lora[e], w_q_up=w_q_up[e],
                         g_kv=g_kv_lora[e], w_kv_up=w_kv_up[e], lq1=lambda_q1[e], lk1=lambda_k1[e],
                         lq2=lambda_q2[e], lk2=lambda_k2[e], g_subln=g_diff_subln[e])
                past = ((cache_mla_ckv[e], cache_mla_kpe[e], cache_diff_k[e], cache_diff_v[e])
                        if grp["has_past"] else None)
                x2d, parts, new = _even_layer(x, grp["past_len"], past, bias, p, lam_init)
                even_new.append(new)
                w_out = w_out_even[e]
            else:
                o = layer // 2
                p = dict(g_attn=g_attn_odd[o], w_in=w_in_odd[o])
                past = ((cache_dsa_k[o], cache_dsa_v[o], cache_dsa_kidx[o])
                        if grp["has_past"] else None)
                x2d, parts, new = _odd_layer(x, grp["past_len"], past, bias, p)
                odd_new.append(new)
                w_out = w_out_odd[o]
            gfin = g_final if layer == depth - 1 else None
            x = _post(x2d, parts, w_out, g_ffn[layer], w_gate[layer], w_up[layer], w_down[layer],
                      gfin, tm).reshape(b, s, D_MODEL)
        results.append((x, [jnp.stack(a) for a in zip(*even_new)], [jnp.stack(a) for a in zip(*odd_new)]))
    (yp, ep, op), (ys, es, os_) = results
    return (yp, ys, *ep, *op, *es, *os_)
```

```python
import functools
import math

import jax
import jax.numpy as jnp
from jax import lax
from jax.experimental import pallas as pl
from jax.experimental.pallas import tpu as pltpu

D_MODEL = 1024
CHUNK = 64
NORM_EPS = 1e-6
SUBLN_EPS = 1e-5
HEADS = 8
MLA_NOPE = 128
MLA_ROPE = 64
MLA_V = 128
Q_LORA = 384
KV_LORA = 256
ROPE_THETA = 10000.0
LOG2E = math.log2(math.e)
MLA_SCALE = (MLA_NOPE + MLA_ROPE) ** -0.5 * LOG2E
DIFF_DIM = 64
DIFF_SCALE = DIFF_DIM ** -0.5 * LOG2E
DSA_DIM = 128
DSA_SCALE = DSA_DIM ** -0.5 * LOG2E
IDX_DIM = 64
IDX_SCALE = IDX_DIM ** -0.5 * HEADS ** -0.5
DSA_TOPK = 256
NUM_BUCKETS = 32
T5_FAR = 91
HEAD_W = 128
MLA_QK = 256
D_FF = -(-8 * D_MODEL // (3 * 256)) * 256
FF_CHUNK = 256
HEADS_PER_STEP = 8
ATT_TILE = 256

LANES = 128
VMEM_LIMIT = 56 * 1024 * 1024

INT_MIN = -2 ** 31
NEG_BIG = -1e30
BF = jnp.bfloat16
F32 = jnp.float32


def _dot(a, b):
    return jnp.dot(a, b, preferred_element_type=F32)


def _dot_nt(a, b):
    return lax.dot_general(a, b, (((1,), (1,)), ((), ())), preferred_element_type=F32)


def _rms(x, g, eps):
    return x * lax.rsqrt(jnp.mean(x * x, axis=-1, keepdims=True) + eps) * g


def _params(sem):
    return pltpu.CompilerParams(dimension_semantics=sem, vmem_limit_bytes=VMEM_LIMIT)


def _const_spec(shape):
    nd = len(shape)
    return pl.BlockSpec(shape, lambda *_: (0,) * nd, pipeline_mode=pl.Buffered(1))


def _even_proj_kernel(x_ref, g_ref, wq_ref, wkv_ref, wd_ref, wkpe_ref, wkrot_ref, gq_ref,
                      wqcat_ref, wqrot_ref, gkv_ref, cos_ref, sin_ref,
                      qcat_ref, ckv_ref, kpe_ref, dq_ref, dk_ref, dv_ref, dk16_ref, dv16_ref):
    h = _rms(x_ref[...], g_ref[...], NORM_EPS).astype(BF)
    cos = cos_ref[...]
    sin = sin_ref[...]
    qn = _rms(_dot(h, wq_ref[...]), gq_ref[...], NORM_EPS).astype(BF)
    qc = _dot(qn, wqcat_ref[...])
    qr = _dot(qn, wqrot_ref[...])
    for hh in range(HEADS):
        a = hh * MLA_QK
        qcat_ref[:, a:a + HEAD_W] = (qc[:, a:a + HEAD_W] * MLA_SCALE).astype(BF)
        pe = qc[:, a + HEAD_W:a + MLA_QK] * cos + qr[:, hh * HEAD_W:(hh + 1) * HEAD_W] * sin
        qcat_ref[:, a + HEAD_W:a + MLA_QK] = (pe * MLA_SCALE).astype(BF)
    ckv_ref[...] = _rms(_dot(h, wkv_ref[...]), gkv_ref[...], NORM_EPS)
    kpe_ref[...] = (_dot(h, wkpe_ref[...]) * cos[:, :MLA_ROPE]
                    + _dot(h, wkrot_ref[...]) * sin[:, :MLA_ROPE])
    dq_ref[...] = (_dot(h, wd_ref[:, 0:1024]) * DIFF_SCALE).astype(BF)
    dk = _dot(h, wd_ref[:, 1024:2048])
    dk_ref[...] = dk
    dk16_ref[...] = dk.astype(BF)
    dv = _dot(h, wd_ref[:, 2048:3072])
    dv_ref[...] = dv
    dv16_ref[...] = dv.astype(BF)


def _even_proj(x2d, seq, tm, g_attn, w_in, g_q, w_q_up, g_kv, cos_p, sin_p):
    t = x2d.shape[0]
    c0, c1, c2 = Q_LORA, Q_LORA + KV_LORA, Q_LORA + KV_LORA + MLA_ROPE
    half = MLA_ROPE // 2
    wq = w_in[:, :c0].astype(BF)
    wkv = w_in[:, c0:c1].astype(BF)
    wkpe = w_in[:, c1:c2]
    wkrot = jnp.concatenate([-wkpe[:, half:], wkpe[:, :half]], axis=1).astype(BF)
    wkpe = wkpe.astype(BF)
    wd = w_in[:, c2:].astype(BF)
    wqu = w_q_up.reshape(Q_LORA, HEADS, MLA_NOPE + MLA_ROPE)
    w_nope, w_pe = wqu[..., :MLA_NOPE], wqu[..., MLA_NOPE:]
    w_rot = jnp.concatenate([-w_pe[..., half:], w_pe[..., :half]], axis=-1)
    zpad = jnp.zeros((Q_LORA, HEADS, MLA_QK - MLA_NOPE - MLA_ROPE), F32)
    wqcat = jnp.concatenate([w_nope, w_pe, zpad], axis=-1).reshape(Q_LORA, HEADS * MLA_QK).astype(BF)
    wqrot = jnp.concatenate([w_rot, zpad], axis=-1).reshape(Q_LORA, HEADS * HEAD_W).astype(BF)
    n_pos = seq // tm
    row = lambda w: pl.BlockSpec((tm, w), lambda i: (i, 0))
    pos = pl.BlockSpec((tm, HEAD_W), lambda i: (i % n_pos, 0))
    out_shapes = [
        jax.ShapeDtypeStruct((t, HEADS * MLA_QK), BF),
        jax.ShapeDtypeStruct((t, KV_LORA), F32),
        jax.ShapeDtypeStruct((t, MLA_ROPE), F32),
        jax.ShapeDtypeStruct((t, 1024), BF),
        jax.ShapeDtypeStruct((t, 1024), F32),
        jax.ShapeDtypeStruct((t, 1024), F32),
        jax.ShapeDtypeStruct((t, 1024), BF),
        jax.ShapeDtypeStruct((t, 1024), BF),
    ]
    ins = [x2d, g_attn.reshape(1, -1), wq, wkv, wd, wkpe, wkrot, g_q.reshape(1, -1), wqcat, wqrot,
           g_kv.reshape(1, -1), cos_p, sin_p]
    in_specs = [row(D_MODEL)] + [_const_spec(a.shape) for a in ins[1:11]] + [pos, pos]
    return pl.pallas_call(
        _even_proj_kernel,
        grid=(t // tm,),
        in_specs=in_specs,
        out_specs=[row(s.shape[1]) for s in out_shapes],
        out_shape=out_shapes,
        compiler_params=_params(("parallel",)),
        name="even_proj",
    )(*ins)


def _kv_up_kernel(ckv_ref, kpe_ref, wk_ref, wv_ref, kcat_ref, v_ref):
    c = ckv_ref[...].astype(BF)
    kn = _dot(c, wk_ref[...])
    v_ref[...] = _dot(c, wv_ref[...]).astype(BF)
    kpe = kpe_ref[...]
    kp = jnp.concatenate([kpe, jnp.zeros_like(kpe)], axis=-1).astype(BF)
    for hh in range(HEADS):
        a = hh * MLA_QK
        kcat_ref[:, a:a + HEAD_W] = kn[:, hh * HEAD_W:(hh + 1) * HEAD_W].astype(BF)
        kcat_ref[:, a + HEAD_W:a + MLA_QK] = kp


def _kv_up(ckv2d, kpe2d, w_kv_up, tm):
    t = ckv2d.shape[0]
    wkv = w_kv_up.reshape(KV_LORA, HEADS, MLA_NOPE + MLA_V)
    wk = wkv[..., :MLA_NOPE].reshape(KV_LORA, HEADS * MLA_NOPE).astype(BF)
    wv = wkv[..., MLA_NOPE:].reshape(KV_LORA, HEADS * MLA_V).astype(BF)
    row = lambda w: pl.BlockSpec((tm, w), lambda i: (i, 0))
    return pl.pallas_call(
        _kv_up_kernel,
        grid=(t // tm,),
        in_specs=[row(KV_LORA), row(MLA_ROPE), _const_spec(wk.shape), _const_spec(wv.shape)],
        out_specs=[row(HEADS * MLA_QK), row(HEADS * MLA_V)],
        out_shape=[jax.ShapeDtypeStruct((t, HEADS * MLA_QK), BF),
                   jax.ShapeDtypeStruct((t, HEADS * MLA_V), BF)],
        compiler_params=_params(("parallel",)),
        name="kv_up",
    )(ckv2d, kpe2d, wk, wv)


def _n_near(tk):
    return 1 + -(-(T5_FAR - 1) // tk)


def _bias_kernel(tab_ref, out_ref, *, tq, tk, n_near):
    h = pl.program_id(0)
    row = lax.broadcasted_iota(jnp.int32, (tq, tk), 0)
    col = lax.broadcasted_iota(jnp.int32, (tq, tk), 1)
    far = tab_ref[NUM_BUCKETS // 2 - 1, h]
    for n in range(1, n_near + 1):
        rel = col - row - (n_near - n) * tk
        dist = jnp.abs(rel)
        large = jnp.full((tq, tk), NUM_BUCKETS // 4, jnp.int32)
        for thr in (12, 16, 23, 32, 46, 64, T5_FAR):
            large = large + jnp.where(dist >= thr, 1, 0)
        bucket = jnp.where(rel > 0, NUM_BUCKETS // 2, 0) + jnp.where(dist < NUM_BUCKETS // 4, dist, large)
        val = jnp.zeros((tq, tk), F32)
        for b in range(NUM_BUCKETS):
            val = jnp.where(bucket == b, tab_ref[b, h], val)
        val = (val - far) * LOG2E
        if n == n_near:
            val = jnp.where((col // CHUNK) <= (row // CHUNK), val, -jnp.inf)
        out_ref[0, n - 1] = val


def _bias_tiles(rel_table, tq, tk):
    n_near = _n_near(tk)
    return pl.pallas_call(
        functools.partial(_bias_kernel, tq=tq, tk=tk, n_near=n_near),
        grid=(HEADS,),
        in_specs=[pl.BlockSpec(memory_space=pltpu.SMEM)],
        out_specs=pl.BlockSpec((1, n_near, tq, tk), lambda h: (h, 0, 0, 0)),
        out_shape=jax.ShapeDtypeStruct((HEADS, n_near, tq, tk), F32),
        compiler_params=_params(("arbitrary",)),
        name="bias_tiles",
    )(rel_table)


def _flash_kernel(*refs, tq, tk, past, n_maps, has_bias, has_sel, lam_init, hg, dq):
    refs = list(refs)
    q_ref, k_ref, v_ref = refs[:3]
    pos = 3
    bias_ref = sel_ref = lam_ref = gsub_ref = None
    if has_bias:
        bias_ref = refs[pos]; pos += 1
    if has_sel:
        sel_ref = refs[pos]; pos += 1
    if n_maps == 2:
        lam_ref, gsub_ref = refs[pos], refs[pos + 1]; pos += 2
    o_ref, m_ref, l_ref, acc_ref = refs[pos:pos + 4]
    n_near = _n_near(tk) if has_bias else 1
    n_col = tk // LANES

    j_diag = past // tk + pl.program_id(2)
    qs = []
    for h in range(hg):
        q = q_ref[0, :, h * dq:(h + 1) * dq]
        if n_maps == 2:
            lane = lax.broadcasted_iota(jnp.int32, q.shape, 1)
            qs.append([jnp.where(lane < DIFF_DIM, q, jnp.zeros_like(q)),
                       jnp.where(lane >= DIFF_DIM, q, jnp.zeros_like(q))])
        else:
            qs.append([q])
    m_ref[...] = jnp.full(m_ref.shape, NEG_BIG, F32)
    l_ref[...] = jnp.zeros(l_ref.shape, F32)
    acc_ref[...] = jnp.zeros(acc_ref.shape, F32)

    def tile(j, near):
        start = pl.multiple_of(j * tk, tk)
        causal = None
        if not has_bias and near is not None:
            row = lax.broadcasted_iota(jnp.int32, (tq, tk), 0)
            col = lax.broadcasted_iota(jnp.int32, (tq, tk), 1)
            causal = jnp.where((col // CHUNK) <= (row // CHUNK), 0.0, -jnp.inf)
        drop = None
        if has_sel:
            drop = sel_ref[0, 0, j].astype(F32)
        for h in range(hg):
            k = k_ref[0, pl.ds(start, tk), h * dq:(h + 1) * dq]
            v = v_ref[0, pl.ds(start, tk), h * HEAD_W:(h + 1) * HEAD_W]
            v1 = jnp.concatenate([v, jnp.ones_like(v)], axis=-1)
            extra = causal
            if has_bias and near is not None:
                extra = bias_ref[h, near]
            for mi in range(n_maps):
                i = h * n_maps + mi
                s = _dot_nt(qs[h][mi], k)
                if extra is not None:
                    s = s + extra
                if drop is not None:
                    s = s + drop
                cols = [s[:, c * LANES:(c + 1) * LANES] for c in range(n_col)]
                mx = cols[0]
                for c in cols[1:]:
                    mx = jnp.maximum(mx, c)
                m_old = m_ref[i]
                m_new = jnp.maximum(m_old, jnp.max(mx, axis=-1, keepdims=True))
                alpha = jnp.exp2(m_old - m_new)
                ps = [jnp.exp2(c - m_new) for c in cols]
                p = ps[0] if len(ps) == 1 else jnp.concatenate(ps, axis=-1)
                pv = _dot(p.astype(BF), v1)
                l_ref[i] = alpha * l_ref[i] + pv[:, HEAD_W:]
                acc_ref[i] = alpha * acc_ref[i] + pv[:, :HEAD_W]
                m_ref[i] = m_new

    def far_body(j, carry):
        tile(j, None)
        return carry

    lax.fori_loop(0, jnp.maximum(j_diag - (n_near - 1), 0), far_body, 0)
    for n in range(n_near):
        j = j_diag - (n_near - 1) + n
        if n == n_near - 1:
            tile(j, n)
        else:
            pl.when(j >= 0)(functools.partial(tile, j, n))

    if n_maps == 2:
        lp = lam_ref[...]
        lam = (jnp.exp(jnp.sum(lp[0:1] * lp[1:2], axis=-1, keepdims=True))
               - jnp.exp(jnp.sum(lp[2:3] * lp[3:4], axis=-1, keepdims=True)) + lam_init)
    for h in range(hg):
        i = h * n_maps
        out = acc_ref[i] * (1.0 / l_ref[i])
        if n_maps == 2:
            out = out - lam * (acc_ref[i + 1] * (1.0 / l_ref[i + 1]))
            out = _rms(out, gsub_ref[...], SUBLN_EPS) * (1.0 - lam_init)
        o_ref[0, :, h * HEAD_W:(h + 1) * HEAD_W] = out.astype(o_ref.dtype)


def _flash(q, k, v, *, tq, past, dq, bias=None, sel=None, lam_params=None, g_subln=None,
           lam_init=0.0):
    b, sq, _ = q.shape
    sk = k.shape[1]
    tk = tq
    n_maps = 2 if lam_params is not None else 1
    hg = HEADS_PER_STEP
    ins = [q, k, v]
    once = pl.Buffered(1)
    in_specs = [
        pl.BlockSpec((1, tq, hg * dq), lambda bi, h, i: (bi, i, h)),
        pl.BlockSpec((1, sk, hg * dq), lambda bi, h, i: (bi, 0, h), pipeline_mode=once),
        pl.BlockSpec((1, sk, hg * HEAD_W), lambda bi, h, i: (bi, 0, h), pipeline_mode=once),
    ]
    if bias is not None:
        ins.append(bias)
        in_specs.append(pl.BlockSpec((hg,) + bias.shape[1:], lambda bi, h, i: (h, 0, 0, 0),
                                     pipeline_mode=once))
    if sel is not None:
        ins.append(sel)
        in_specs.append(pl.BlockSpec((1, 1) + sel.shape[2:], lambda bi, h, i: (bi, i, 0, 0, 0)))
    if n_maps == 2:
        ins += [lam_params, g_subln.reshape(1, -1)]
        in_specs += [pl.BlockSpec(lam_params.shape, lambda bi, h, i: (0, 0)),
                     pl.BlockSpec((1, HEAD_W), lambda bi, h, i: (0, 0))]
    kern = functools.partial(_flash_kernel, tq=tq, tk=tk, past=past, n_maps=n_maps,
                             has_bias=bias is not None, has_sel=sel is not None,
                             lam_init=lam_init, hg=hg, dq=dq)
    return pl.pallas_call(
        kern,
        grid=(b, HEADS // hg, sq // tq),
        in_specs=in_specs,
        out_specs=pl.BlockSpec((1, tq, hg * HEAD_W), lambda bi, h, i: (bi, i, h)),
        out_shape=jax.ShapeDtypeStruct((b, sq, HEADS * HEAD_W), BF),
        scratch_shapes=[pltpu.VMEM((hg * n_maps, tq, LANES), F32), pltpu.VMEM((hg * n_maps, tq, LANES), F32),
                        pltpu.VMEM((hg * n_maps, tq, HEAD_W), F32)],
        compiler_params=_params(("parallel", "parallel", "arbitrary")),
        name="flash",
    )(*ins)


def _post_kernel(*refs, n_parts, final):
    x_ref = refs[0]
    part_refs = refs[1:1 + n_parts]
    wout_refs = refs[1 + n_parts:1 + 2 * n_parts]
    gffn_ref, wg_ref, wu_ref, wd_ref = refs[1 + 2 * n_parts:5 + 2 * n_parts]
    gfin_ref = refs[5 + 2 * n_parts] if final else None
    o_ref = refs[-1]
    x = x_ref[...]
    for p in range(n_parts):
        x = x + _dot(part_refs[p][...], wout_refs[p][...])
    h = _rms(x, gffn_ref[...], NORM_EPS).astype(BF)
    o_ref[...] = x
    for c in range(D_FF // FF_CHUNK):
        sl = slice(c * FF_CHUNK, (c + 1) * FF_CHUNK)
        g = _dot(h, wg_ref[:, sl])
        u = _dot(h, wu_ref[:, sl])
        a = (g * (1.0 / (1.0 + jnp.exp(-g))) * u).astype(BF)
        o_ref[...] += _dot(a, wd_ref[sl, :])
    if final:
        o_ref[...] = _rms(o_ref[...], gfin_ref[...], NORM_EPS)


def _post(x2d, parts, w_out, g_ffn, w_gate, w_up, w_down, g_final, tm):
    t = x2d.shape[0]
    final = g_final is not None
    row = lambda w: pl.BlockSpec((tm, w), lambda i: (i, 0))
    w_out = w_out.astype(BF)
    consts = [w_out[p * 1024:(p + 1) * 1024] for p in range(len(parts))]
    consts += [g_ffn.reshape(1, -1), w_gate.astype(BF), w_up.astype(BF), w_down.astype(BF)]
    if final:
        consts.append(g_final.reshape(1, -1))
    return pl.pallas_call(
        functools.partial(_post_kernel, n_parts=len(parts), final=final),
        grid=(t // tm,),
        in_specs=[row(D_MODEL)] + [row(1024) for _ in parts] + [_const_spec(c.shape) for c in consts],
        out_specs=row(D_MODEL),
        out_shape=jax.ShapeDtypeStruct((t, D_MODEL), F32),
        compiler_params=_params(("parallel",)),
        name="post",
    )(x2d, *parts, *consts)


def _odd_proj_kernel(x_ref, g_ref, wm_ref, wki_ref, ww_ref,
                     q_ref, k_ref, v_ref, k16_ref, v16_ref, qi_ref, ki_ref, ki16_ref, w_out_ref):
    h = _rms(x_ref[...], g_ref[...], NORM_EPS).astype(BF)
    q_ref[...] = (_dot(h, wm_ref[:, 0:1024]) * DSA_SCALE).astype(BF)
    k = _dot(h, wm_ref[:, 1024:2048])
    k_ref[...] = k
    k16_ref[...] = k.astype(BF)
    v = _dot(h, wm_ref[:, 2048:3072])
    v_ref[...] = v
    v16_ref[...] = v.astype(BF)
    qi_ref[...] = _dot(h, wm_ref[:, 3072:3584]).astype(BF)
    ki = _dot(h, wki_ref[...])
    ki_ref[...] = ki
    ki16_ref[...] = ki.astype(BF)
    w_out_ref[...] = _dot(h, ww_ref[...]) * IDX_SCALE


def _odd_proj(x2d, tm, g_attn, w_in):
    t = x2d.shape[0]
    c0 = 3 * 1024 + HEADS * IDX_DIM
    wm = w_in[:, :c0].astype(BF)
    wki = w_in[:, c0:c0 + IDX_DIM].astype(BF)
    ww = jnp.pad(w_in[:, c0 + IDX_DIM:], ((0, 0), (0, LANES - HEADS))).astype(BF)
    row = lambda w: pl.BlockSpec((tm, w), lambda i: (i, 0))
    out_shapes = [
        jax.ShapeDtypeStruct((t, 1024), BF),
        jax.ShapeDtypeStruct((t, 1024), F32),
        jax.ShapeDtypeStruct((t, 1024), F32),
        jax.ShapeDtypeStruct((t, 1024), BF),
        jax.ShapeDtypeStruct((t, 1024), BF),
        jax.ShapeDtypeStruct((t, HEADS * IDX_DIM), BF),
        jax.ShapeDtypeStruct((t, IDX_DIM), F32),
        jax.ShapeDtypeStruct((t, IDX_DIM), BF),
        jax.ShapeDtypeStruct((t, LANES), F32),
    ]
    return pl.pallas_call(
        _odd_proj_kernel,
        grid=(t // tm,),
        in_specs=[row(D_MODEL), _const_spec((1, D_MODEL)), _const_spec(wm.shape),
                  _const_spec(wki.shape), _const_spec(ww.shape)],
        out_specs=[row(s.shape[1]) for s in out_shapes],
        out_shape=out_shapes,
        compiler_params=_params(("parallel",)),
        name="odd_proj",
    )(x2d, g_attn.reshape(1, -1), wm, wki, ww)


def _indexer_kernel(qi_ref, w_ref, k_ref, sel_ref, key_ref, wb_ref, cut_ref, *, tq, tk, past, top_k,
                    nk, idx_bits):
    qblk = pl.program_id(1)
    nkv = past // tk + qblk + 1
    q0 = past + qblk * tq
    n_col = tk // LANES
    q = qi_ref[0]
    w = w_ref[0]
    qh = [q[:, h * IDX_DIM:(h + 1) * IDX_DIM] for h in range(HEADS)]
    for h in range(HEADS):
        wb_ref[h] = jnp.broadcast_to(w[:, h:h + 1], (tq, LANES))
    row = lax.broadcasted_iota(jnp.int32, (tq, LANES), 0)
    lane = lax.broadcasted_iota(jnp.int32, (tq, LANES), 1)
    row_chunk = (q0 + row) // CHUNK
    kf = float(top_k)
    cols = [slice(c * LANES, (c + 1) * LANES) for c in range(n_col)]

    def score_body(j, carry):
        k = k_ref[0, pl.ds(pl.multiple_of(j * tk, tk), tk), :]
        relu = [jnp.maximum(_dot_nt(qh[h], k), 0.0) for h in range(HEADS)]
        for c, sl in enumerate(cols):
            sc = jnp.zeros((tq, LANES), F32)
            for h in range(HEADS):
                sc = sc + wb_ref[h] * relu[h][:, sl]
            sc = jnp.where(sc == 0.0, 0.0, sc)
            bits = pltpu.bitcast(sc, jnp.int32)
            key = jnp.where(bits < 0, bits ^ 0x7FFFFFFF, bits)
            ok = ((j * tk + c * LANES + lane) // CHUNK <= row_chunk) & (sc > -jnp.inf)
            key_ref[j, :, sl] = jnp.where(ok, key, INT_MIN)
        return carry

    lax.fori_loop(0, nkv, score_body, 0)

    def count(pred):
        def body(j, acc):
            for c, sl in enumerate(cols):
                acc = acc + jnp.where(pred(key_ref[j, :, sl], j * tk + c * LANES + lane), 1.0, 0.0)
            return acc
        acc = lax.fori_loop(0, nkv, body, jnp.zeros((tq, LANES), F32))
        return jnp.broadcast_to(jnp.sum(acc, axis=-1, keepdims=True), (tq, LANES))

    c0 = count(lambda key, idx: key >= 0)
    thr = jnp.where(c0 >= kf, 0, INT_MIN).astype(jnp.int32)

    def bit_body(i, thr):
        cand = thr + lax.shift_left(jnp.int32(1), 30 - i)
        c = count(lambda key, idx: key >= cand)
        return jnp.where(c >= kf, cand, thr)

    thr = lax.fori_loop(0, 31, bit_body, thr)

    c_gt = count(lambda key, idx: key > thr)
    c_ge = count(lambda key, idx: key >= thr)
    need = kf - c_gt
    real = thr > INT_MIN
    excess = real & (c_ge - c_gt > need)
    any_excess = jnp.max(jnp.where(excess, 1.0, 0.0)) > 0.0

    def emit(keep_fn):
        def body(j, carry):
            for c, sl in enumerate(cols):
                keep = keep_fn(key_ref[j, :, sl], j * tk + c * LANES + lane)
                sel_ref[0, 0, j, :, sl] = jnp.where(keep, 0.0, -jnp.inf).astype(BF)
            return carry
        lax.fori_loop(0, nkv, body, 0)

    @pl.when(jnp.logical_not(any_excess))
    def _():
        thr1 = jnp.maximum(thr, INT_MIN + 1)
        emit(lambda key, idx: key >= thr1)

    @pl.when(any_excess)
    def _():
        def idx_body(i, cut):
            cand = cut + lax.shift_left(jnp.int32(1), idx_bits - 1 - i)
            c = count(lambda key, idx: (key == thr) & (idx < cand))
            return jnp.where(c < need, cand, cut)
        cut_ref[...] = lax.fori_loop(0, idx_bits, idx_body, jnp.zeros((tq, LANES), jnp.int32))
        cut = cut_ref[...]
        emit(lambda key, idx: (key > thr) | ((key == thr) & real & (idx <= cut)))

    def zero_body(j, carry):
        sel_ref[0, 0, j] = jnp.zeros((tq, tk), BF)
        return carry

    lax.fori_loop(nkv, nk, zero_body, 0)


def _indexer(qidx, widx, kidx16, *, tq, past, top_k):
    b, sq, _ = qidx.shape
    sk = kidx16.shape[1]
    tk = tq
    nk = sk // tk
    idx_bits = max(1, (sk - 1).bit_length())
    kern = functools.partial(_indexer_kernel, tq=tq, tk=tk, past=past, top_k=top_k, nk=nk,
                             idx_bits=idx_bits)
    return pl.pallas_call(
        kern,
        grid=(b, sq // tq),
        in_specs=[pl.BlockSpec((1, tq, HEADS * IDX_DIM), lambda bi, i: (bi, i, 0)),
                  pl.BlockSpec((1, tq, LANES), lambda bi, i: (bi, i, 0)),
                  pl.BlockSpec((1, sk, IDX_DIM), lambda bi, i: (bi, 0, 0))],
        out_specs=pl.BlockSpec((1, 1, nk, tq, tk), lambda bi, i: (bi, i, 0, 0, 0)),
        out_shape=jax.ShapeDtypeStruct((b, sq // tq, nk, tq, tk), BF),
        scratch_shapes=[pltpu.VMEM((nk, tq, tk), jnp.int32), pltpu.VMEM((HEADS, tq, LANES), F32),
                        pltpu.VMEM((tq, LANES), jnp.int32)],
        compiler_params=_params(("parallel", "arbitrary")),
        name="indexer",
    )(qidx, widx, kidx16)


def _rope_tables(pos):
    half = MLA_ROPE // 2
    inv = ROPE_THETA ** (-jnp.arange(half, dtype=F32) / half)
    ang = pos.astype(F32)[:, None] * inv[None, :]
    pad = jnp.zeros((pos.shape[0], HEAD_W - MLA_ROPE), F32)
    cos, sin = jnp.cos(ang), jnp.sin(ang)
    return (jnp.concatenate([cos, cos, pad], axis=1), jnp.concatenate([sin, sin, pad], axis=1))


def _row_tile(seq):
    return ATT_TILE if seq % ATT_TILE == 0 else CHUNK


def _pad_seq(a):
    n = -a.shape[1] % ATT_TILE
    return a if n == 0 else jnp.pad(a, ((0, 0), (0, n), (0, 0)))


def _even_layer(x, past_len, past, bias, p, lam_init):
    b, s, _ = x.shape
    tm = _row_tile(s)
    pos = past_len + jnp.arange(s, dtype=jnp.int32)
    cos_p, sin_p = _rope_tables(pos)
    x2d = x.reshape(b * s, D_MODEL)
    qcat, ckv, kpe, dq, dk, dv, dk16, dv16 = _even_proj(
        x2d, s, tm, p["g_attn"], p["w_in"], p["g_q"], p["w_q_up"], p["g_kv"], cos_p, sin_p)
    r3 = lambda a: a.reshape(b, s, -1)
    if past is None:
        ckv_all, kpe_all, dk_all, dv_all = r3(ckv), r3(kpe), r3(dk16), r3(dv16)
    else:
        c_ckv, c_kpe, c_dk, c_dv = past
        ckv_all = jnp.concatenate([c_ckv, r3(ckv)], axis=1)
        kpe_all = jnp.concatenate([c_kpe, r3(kpe)], axis=1)
        dk_all = jnp.concatenate([c_dk.reshape(b, past_len, -1).astype(BF), r3(dk16)], axis=1)
        dv_all = jnp.concatenate([c_dv.reshape(b, past_len, -1).astype(BF), r3(dv16)], axis=1)
    ckv_all, kpe_all, dk_all, dv_all = map(_pad_seq, (ckv_all, kpe_all, dk_all, dv_all))
    sk = ckv_all.shape[1]
    kcat, vmla = _kv_up(ckv_all.reshape(b * sk, KV_LORA), kpe_all.reshape(b * sk, MLA_ROPE),
                        p["w_kv_up"], ATT_TILE)
    a_out = _flash(_pad_seq(r3(qcat)), kcat.reshape(b, sk, -1), vmla.reshape(b, sk, -1),
                   tq=ATT_TILE, past=past_len, dq=MLA_QK)
    lam_params = jnp.stack([p["lq1"], p["lk1"], p["lq2"], p["lk2"]]).astype(F32)
    b_out = _flash(_pad_seq(r3(dq)), dk_all, dv_all, tq=ATT_TILE, past=past_len, dq=HEAD_W, bias=bias,
                   lam_params=lam_params, g_subln=p["g_subln"], lam_init=lam_init)
    parts = [a_out[:, :s].reshape(b * s, -1), b_out[:, :s].reshape(b * s, -1)]
    new = (r3(ckv), r3(kpe), dk.reshape(b, s, HEADS, 2 * DIFF_DIM), dv.reshape(b, s, HEADS, 2 * DIFF_DIM))
    return x2d, parts, new


def _odd_layer(x, past_len, past, bias, p):
    b, s, _ = x.shape
    tm = _row_tile(s)
    x2d = x.reshape(b * s, D_MODEL)
    q, k, v, k16, v16, qi, ki, ki16, widx = _odd_proj(x2d, tm, p["g_attn"], p["w_in"])
    r3 = lambda a: a.reshape(b, s, -1)
    if past is None:
        k_all, v_all, ki_all = r3(k16), r3(v16), r3(ki16)
    else:
        c_k, c_v, c_ki = past
        k_all = jnp.concatenate([c_k.reshape(b, past_len, -1).astype(BF), r3(k16)], axis=1)
        v_all = jnp.concatenate([c_v.reshape(b, past_len, -1).astype(BF), r3(v16)], axis=1)
        ki_all = jnp.concatenate([c_ki.astype(BF), r3(ki16)], axis=1)
    top_k = min(DSA_TOPK, k_all.shape[1] // 4)
    k_all, v_all, ki_all = map(_pad_seq, (k_all, v_all, ki_all))
    sel = _indexer(_pad_seq(r3(qi)), _pad_seq(r3(widx)), ki_all, tq=ATT_TILE, past=past_len, top_k=top_k)
    out = _flash(_pad_seq(r3(q)), k_all, v_all, tq=ATT_TILE, past=past_len, dq=HEAD_W, bias=bias, sel=sel)
    new = (k.reshape(b, s, HEADS, DSA_DIM), v.reshape(b, s, HEADS, DSA_DIM), r3(ki))
    return x2d, [out[:, :s].reshape(b * s, -1)], new


def kernel(x_prompt, x_sample, cache_mla_ckv, cache_mla_kpe, cache_diff_k, cache_diff_v, cache_dsa_k, cache_dsa_v, cache_dsa_kidx, g_attn_even, w_in_even, g_q_lora, w_q_up, g_kv_lora, w_kv_up, lambda_q1, lambda_k1, lambda_q2, lambda_k2, g_diff_subln, w_out_even, g_attn_odd, w_in_odd, w_out_odd, rel_bias_table, g_ffn, w_gate, w_up, w_down, g_final):
    depth = g_ffn.shape[0]
    past_len = cache_mla_ckv.shape[2]
    groups = [
        dict(x=x_prompt, past_len=0, has_past=False),
        dict(x=x_sample, past_len=past_len, has_past=True),
    ]
    bias = _bias_tiles(rel_bias_table, ATT_TILE, ATT_TILE)
    results = []
    for grp in groups:
        x = grp["x"]
        b, s, _ = x.shape
        tm = _row_tile(s)
        even_new, odd_new = [], []
        for layer in range(depth):
            if layer % 2 == 0:
                e = layer // 2
                lam_init = 0.8 - 0.6 * math.exp(-0.3 * layer)
                p = dict(g_attn=g_attn_even[e], w_in=w_in_even[e], g_q=g_q_lora[e], w_q_up=w_q_up[e],
                         g_kv=g_kv_lora[e], w_kv_up=w_kv_up[e], lq1=lambda_q1[e], lk1=lambda_k1[e],
                         lq2=lambda_q2[e], lk2=lambda_k2[e], g_subln=g_diff_subln[e])
                past = ((cache_mla_ckv[e], cache_mla_kpe[e], cache_diff_k[e], cache_diff_v[e])
                        if grp["has_past"] else None)
                x2d, parts, new = _even_layer(x, grp["past_len"], past, bias, p, lam_init)
                even_new.append(new)
                w_out = w_out_even[e]
            else:
                o = layer // 2
                p = dict(g_attn=g_attn_odd[o], w_in=w_in_odd[o])
                past = ((cache_dsa_k[o], cache_dsa_v[o], cache_dsa_kidx[o])
                        if grp["has_past"] else None)
                x2d, parts, new = _odd_layer(x, grp["past_len"], past, bias, p)
                odd_new.append(new)
                w_out = w_out_odd[o]
            gfin = g_final if layer == depth - 1 else None
            x = _post(x2d, parts, w_out, g_ffn[layer], w_gate[layer], w_up[layer], w_down[layer],
                      gfin, tm).reshape(b, s, D_MODEL)
        results.append((x, [jnp.stack(a) for a in zip(*even_new)], [jnp.stack(a) for a in zip(*odd_new)]))
    (yp, ep, op), (ys, es, os_) = results
    return (yp, ys, *ep, *op, *es, *os_)
```

```python
import functools
import math

import jax
import jax.numpy as jnp
from jax import lax
from jax.experimental import pallas as pl
from jax.experimental.pallas import tpu as pltpu

D_MODEL = 1024
CHUNK = 64
NORM_EPS = 1e-6
SUBLN_EPS = 1e-5
HEADS = 8
MLA_NOPE = 128
MLA_ROPE = 64
MLA_V = 128
Q_LORA = 384
KV_LORA = 256
ROPE_THETA = 10000.0
LOG2E = math.log2(math.e)
MLA_SCALE = (MLA_NOPE + MLA_ROPE) ** -0.5 * LOG2E
DIFF_DIM = 64
DIFF_SCALE = DIFF_DIM ** -0.5 * LOG2E
DSA_DIM = 128
DSA_SCALE = DSA_DIM ** -0.5 * LOG2E
IDX_DIM = 64
IDX_SCALE = IDX_DIM ** -0.5 * HEADS ** -0.5
DSA_TOPK = 256
NUM_BUCKETS = 32
T5_FAR = 91
HEAD_W = 128
MLA_QK = 256
D_FF = -(-8 * D_MODEL // (3 * 256)) * 256
FF_CHUNK = 256
HEADS_PER_STEP = 8
ATT_TILE = 256
ONES_ROWS = 16

LANES = 128
VMEM_LIMIT = 56 * 1024 * 1024

INT_MIN = -2 ** 31
KEY_NEG_INF = (0xFF800000 ^ 0x7FFFFFFF) - 2 ** 32
ACC_ROWS = 32
NEG_BIG = -1e30
BF = jnp.bfloat16
F32 = jnp.float32


def _dot(a, b):
    return jnp.dot(a, b, preferred_element_type=F32)


def _dot_nt(a, b):
    return lax.dot_general(a, b, (((1,), (1,)), ((), ())), preferred_element_type=F32)


def _rms(x, g, eps):
    return x * lax.rsqrt(jnp.mean(x * x, axis=-1, keepdims=True) + eps) * g


def _params(sem):
    return pltpu.CompilerParams(dimension_semantics=sem, vmem_limit_bytes=VMEM_LIMIT)


def _const_spec(shape):
    nd = len(shape)
    return pl.BlockSpec(shape, lambda *_: (0,) * nd, pipeline_mode=pl.Buffered(1))


def _even_proj_kernel(x_ref, g_ref, wq_ref, wkv_ref, wd_ref, wkpe_ref, wkrot_ref, gq_ref,
                      wqcat_ref, wqrot_ref, gkv_ref, cos_ref, sin_ref,
                      qcat_ref, ckv_ref, kpe_ref, dq_ref, dk_ref, dv_ref, dk16_ref, dv16_ref):
    h = _rms(x_ref[...], g_ref[...], NORM_EPS).astype(BF)
    cos = cos_ref[...]
    sin = sin_ref[...]
    qn = _rms(_dot(h, wq_ref[...]), gq_ref[...], NORM_EPS).astype(BF)
    qc = _dot(qn, wqcat_ref[...])
    qr = _dot(qn, wqrot_ref[...])
    for hh in range(HEADS):
        a = hh * MLA_QK
        qcat_ref[:, a:a + HEAD_W] = (qc[:, a:a + HEAD_W] * MLA_SCALE).astype(BF)
        pe = qc[:, a + HEAD_W:a + MLA_QK] * cos + qr[:, hh * HEAD_W:(hh + 1) * HEAD_W] * sin
        qcat_ref[:, a + HEAD_W:a + MLA_QK] = (pe * MLA_SCALE).astype(BF)
    ckv_ref[...] = _rms(_dot(h, wkv_ref[...]), gkv_ref[...], NORM_EPS)
    kpe_ref[...] = (_dot(h, wkpe_ref[...]) * cos[:, :MLA_ROPE]
                    + _dot(h, wkrot_ref[...]) * sin[:, :MLA_ROPE])
    dq_ref[...] = (_dot(h, wd_ref[:, 0:1024]) * DIFF_SCALE).astype(BF)
    dk = _dot(h, wd_ref[:, 1024:2048])
    dk_ref[...] = dk
    dk16_ref[...] = dk.astype(BF)
    dv = _dot(h, wd_ref[:, 2048:3072])
    dv_ref[...] = dv
    dv16_ref[...] = dv.astype(BF)


def _even_proj(x2d, seq, tm, g_attn, w_in, g_q, w_q_up, g_kv, cos_p, sin_p):
    t = x2d.shape[0]
    c0, c1, c2 = Q_LORA, Q_LORA + KV_LORA, Q_LORA + KV_LORA + MLA_ROPE
    half = MLA_ROPE // 2
    wq = w_in[:, :c0].astype(BF)
    wkv = w_in[:, c0:c1].astype(BF)
    wkpe = w_in[:, c1:c2]
    wkrot = jnp.concatenate([-wkpe[:, half:], wkpe[:, :half]], axis=1).astype(BF)
    wkpe = wkpe.astype(BF)
    wd = w_in[:, c2:].astype(BF)
    wqu = w_q_up.reshape(Q_LORA, HEADS, MLA_NOPE + MLA_ROPE)
    w_nope, w_pe = wqu[..., :MLA_NOPE], wqu[..., MLA_NOPE:]
    w_rot = jnp.concatenate([-w_pe[..., half:], w_pe[..., :half]], axis=-1)
    zpad = jnp.zeros((Q_LORA, HEADS, MLA_QK - MLA_NOPE - MLA_ROPE), F32)
    wqcat = jnp.concatenate([w_nope, w_pe, zpad], axis=-1).reshape(Q_LORA, HEADS * MLA_QK).astype(BF)
    wqrot = jnp.concatenate([w_rot, zpad], axis=-1).reshape(Q_LORA, HEADS * HEAD_W).astype(BF)
    n_pos = seq // tm
    row = lambda w: pl.BlockSpec((tm, w), lambda i: (i, 0))
    pos = pl.BlockSpec((tm, HEAD_W), lambda i: (i % n_pos, 0))
    out_shapes = [
        jax.ShapeDtypeStruct((t, HEADS * MLA_QK), BF),
        jax.ShapeDtypeStruct((t, KV_LORA), F32),
        jax.ShapeDtypeStruct((t, MLA_ROPE), F32),
        jax.ShapeDtypeStruct((t, 1024), BF),
        jax.ShapeDtypeStruct((t, 1024), F32),
        jax.ShapeDtypeStruct((t, 1024), F32),
        jax.ShapeDtypeStruct((t, 1024), BF),
        jax.ShapeDtypeStruct((t, 1024), BF),
    ]
    ins = [x2d, g_attn.reshape(1, -1), wq, wkv, wd, wkpe, wkrot, g_q.reshape(1, -1), wqcat, wqrot,
           g_kv.reshape(1, -1), cos_p, sin_p]
    in_specs = [row(D_MODEL)] + [_const_spec(a.shape) for a in ins[1:11]] + [pos, pos]
    return pl.pallas_call(
        _even_proj_kernel,
        grid=(t // tm,),
        in_specs=in_specs,
        out_specs=[row(s.shape[1]) for s in out_shapes],
        out_shape=out_shapes,
        compiler_params=_params(("parallel",)),
        name="even_proj",
    )(*ins)


def _kv_up_kernel(ckv_ref, kpe_ref, wk_ref, wv_ref, kcat_ref, v_ref):
    c = ckv_ref[...].astype(BF)
    kn = _dot(c, wk_ref[...])
    v_ref[...] = _dot(c, wv_ref[...]).astype(BF)
    kpe = kpe_ref[...]
    kp = jnp.concatenate([kpe, jnp.zeros_like(kpe)], axis=-1).astype(BF)
    for hh in range(HEADS):
        a = hh * MLA_QK
        kcat_ref[:, a:a + HEAD_W] = kn[:, hh * HEAD_W:(hh + 1) * HEAD_W].astype(BF)
        kcat_ref[:, a + HEAD_W:a + MLA_QK] = kp


def _kv_up(ckv2d, kpe2d, w_kv_up, tm):
    t = ckv2d.shape[0]
    wkv = w_kv_up.reshape(KV_LORA, HEADS, MLA_NOPE + MLA_V)
    wk = wkv[..., :MLA_NOPE].reshape(KV_LORA, HEADS * MLA_NOPE).astype(BF)
    wv = wkv[..., MLA_NOPE:].reshape(KV_LORA, HEADS * MLA_V).astype(BF)
    row = lambda w: pl.BlockSpec((tm, w), lambda i: (i, 0))
    return pl.pallas_call(
        _kv_up_kernel,
        grid=(t // tm,),
        in_specs=[row(KV_LORA), row(MLA_ROPE), _const_spec(wk.shape), _const_spec(wv.shape)],
        out_specs=[row(HEADS * MLA_QK), row(HEADS * MLA_V)],
        out_shape=[jax.ShapeDtypeStruct((t, HEADS * MLA_QK), BF),
                   jax.ShapeDtypeStruct((t, HEADS * MLA_V), BF)],
        compiler_params=_params(("parallel",)),
        name="kv_up",
    )(ckv2d, kpe2d, wk, wv)


def _n_near(tk):
    return 1 + -(-(T5_FAR - 1) // tk)


def _bias_kernel(tab_ref, out_ref, *, tq, tk, n_near):
    h = pl.program_id(0)
    col = lax.broadcasted_iota(jnp.int32, (tk, tq), 0)
    row = lax.broadcasted_iota(jnp.int32, (tk, tq), 1)
    far = tab_ref[NUM_BUCKETS // 2 - 1, h]
    for n in range(1, n_near + 1):
        rel = col - row - (n_near - n) * tk
        dist = jnp.abs(rel)
        large = jnp.full((tk, tq), NUM_BUCKETS // 4, jnp.int32)
        for thr in (12, 16, 23, 32, 46, 64, T5_FAR):
            large = large + jnp.where(dist >= thr, 1, 0)
        bucket = jnp.where(rel > 0, NUM_BUCKETS // 2, 0) + jnp.where(dist < NUM_BUCKETS // 4, dist, large)
        val = jnp.zeros((tk, tq), F32)
        for b in range(NUM_BUCKETS):
            val = jnp.where(bucket == b, tab_ref[b, h], val)
        val = (val - far) * LOG2E
        if n == n_near:
            val = jnp.where((col // CHUNK) <= (row // CHUNK), val, -jnp.inf)
        out_ref[0, n - 1] = val


def _bias_tiles(rel_table, tq, tk):
    n_near = _n_near(tk)
    return pl.pallas_call(
        functools.partial(_bias_kernel, tq=tq, tk=tk, n_near=n_near),
        grid=(HEADS,),
        in_specs=[pl.BlockSpec(memory_space=pltpu.SMEM)],
        out_specs=pl.BlockSpec((1, n_near, tk, tq), lambda h: (h, 0, 0, 0)),
        out_shape=jax.ShapeDtypeStruct((HEADS, n_near, tk, tq), F32),
        compiler_params=_params(("arbitrary",)),
        name="bias_tiles",
    )(rel_table)


def _flash_kernel(*refs, tq, tk, past, n_maps, has_bias, has_sel, lam_init, hg, dq):
    refs = list(refs)
    q_ref, k_ref, vt_ref = refs[:3]
    pos = 3
    bias_ref = sel_ref = lam_ref = gsub_ref = None
    if has_bias:
        bias_ref = refs[pos]; pos += 1
    if has_sel:
        sel_ref = refs[pos]; pos += 1
    if n_maps == 2:
        lam_ref, gsub_ref = refs[pos], refs[pos + 1]; pos += 2
    o_ref, m_ref, l_ref, acc_ref = refs[pos:pos + 4]
    n_near = _n_near(tk) if has_bias else 1
    ntq = n_maps * tq

    j_diag = past // tk + pl.program_id(2)
    qs = []
    for h in range(hg):
        q = q_ref[0, :, h * dq:(h + 1) * dq]
        if n_maps == 2:
            lane = lax.broadcasted_iota(jnp.int32, q.shape, 1)
            q = jnp.concatenate([jnp.where(lane < DIFF_DIM, q, jnp.zeros_like(q)),
                                 jnp.where(lane >= DIFF_DIM, q, jnp.zeros_like(q))], axis=0)
        qs.append(q)
    m_ref[...] = jnp.full(m_ref.shape, NEG_BIG, F32)
    l_ref[...] = jnp.zeros(l_ref.shape, F32)
    acc_ref[...] = jnp.zeros(acc_ref.shape, F32)
    ones = jnp.ones((ONES_ROWS, tk), BF)

    def tile(j, near):
        start = pl.multiple_of(j * tk, tk)
        extra = None
        if not has_bias and near is not None:
            krow = lax.broadcasted_iota(jnp.int32, (tk, tq), 0)
            qcol = lax.broadcasted_iota(jnp.int32, (tk, tq), 1)
            extra = jnp.where((krow // CHUNK) <= (qcol // CHUNK), 0.0, -jnp.inf)
        if has_sel:
            extra = sel_ref[0, 0, j].astype(F32)
        ss = []
        for h in range(hg):
            k = k_ref[0, pl.ds(start, tk), h * dq:(h + 1) * dq]
            ss.append(_dot_nt(k, qs[h]))
        pp, aa = [], []
        for h in range(hg):
            add = extra
            if has_bias and near is not None:
                add = bias_ref[h, near] if add is None else add + bias_ref[h, near]
            s = ss[h]
            if add is not None:
                s = s + (add if n_maps == 1 else jnp.concatenate([add] * n_maps, axis=1))
            m_old = m_ref[h]
            m_new = jnp.maximum(m_old, jnp.max(s, axis=0, keepdims=True))
            aa.append(jnp.exp2(m_old - m_new))
            pp.append(jnp.exp2(s - m_new[:1]).astype(BF))
            m_ref[h] = m_new
        for h in range(hg):
            v1t = jnp.concatenate([vt_ref[j, h * HEAD_W:(h + 1) * HEAD_W, :], ones], axis=0)
            pv = _dot(v1t, pp[h])
            l_ref[h] = aa[h] * l_ref[h] + pv[HEAD_W:HEAD_W + 8]
            acc_ref[h] = aa[h][:1] * acc_ref[h] + pv[:HEAD_W]

    def far_body(j, carry):
        tile(j, None)
        return carry

    lax.fori_loop(0, jnp.maximum(j_diag - (n_near - 1), 0), far_body, 0)
    for n in range(n_near):
        j = j_diag - (n_near - 1) + n
        if n == n_near - 1:
            tile(j, n)
        else:
            pl.when(j >= 0)(functools.partial(tile, j, n))

    if n_maps == 2:
        lp = lam_ref[...]
        lam = (jnp.exp(jnp.sum(lp[0:1] * lp[1:2], axis=-1, keepdims=True))
               - jnp.exp(jnp.sum(lp[2:3] * lp[3:4], axis=-1, keepdims=True)) + lam_init)
    for h in range(hg):
        out = acc_ref[h] * (1.0 / l_ref[h][:1])
        if n_maps == 2:
            out = out[:, :tq] - lam * out[:, tq:]
            ms = jnp.mean(out * out, axis=0, keepdims=True)
            out = out * lax.rsqrt(ms + SUBLN_EPS) * gsub_ref[...] * (1.0 - lam_init)
        o_ref[0, :, h * HEAD_W:(h + 1) * HEAD_W] = out.T.astype(o_ref.dtype)


def _flash(q, k, v, *, tq, past, dq, bias=None, sel=None, lam_params=None, g_subln=None,
           lam_init=0.0):
    b, sq, _ = q.shape
    sk = k.shape[1]
    tk = tq
    nk = sk // tk
    n_maps = 2 if lam_params is not None else 1
    hg = HEADS_PER_STEP
    vt = v.reshape(b, nk, tk, HEADS * HEAD_W).transpose(0, 1, 3, 2).reshape(b * nk, HEADS * HEAD_W, tk)
    ins = [q, k, vt]
    once = pl.Buffered(1)
    in_specs = [
        pl.BlockSpec((1, tq, hg * dq), lambda bi, h, i: (bi, i, h)),
        pl.BlockSpec((1, sk, hg * dq), lambda bi, h, i: (bi, 0, h), pipeline_mode=once),
        pl.BlockSpec((nk, hg * HEAD_W, tk), lambda bi, h, i: (bi, h, 0), pipeline_mode=once),
    ]
    if bias is not None:
        ins.append(bias)
        in_specs.append(pl.BlockSpec((hg,) + bias.shape[1:], lambda bi, h, i: (h, 0, 0, 0),
                                     pipeline_mode=once))
    if sel is not None:
        ins.append(sel)
        in_specs.append(pl.BlockSpec((1, 1) + sel.shape[2:], lambda bi, h, i: (bi, i, 0, 0, 0)))
    if n_maps == 2:
        ins += [lam_params, jnp.broadcast_to(g_subln.astype(F32)[:, None], (HEAD_W, tq))]
        in_specs += [pl.BlockSpec(lam_params.shape, lambda bi, h, i: (0, 0)),
                     pl.BlockSpec((HEAD_W, tq), lambda bi, h, i: (0, 0))]
    kern = functools.partial(_flash_kernel, tq=tq, tk=tk, past=past, n_maps=n_maps,
                             has_bias=bias is not None, has_sel=sel is not None,
                             lam_init=lam_init, hg=hg, dq=dq)
    return pl.pallas_call(
        kern,
        grid=(b, HEADS // hg, sq // tq),
        in_specs=in_specs,
        out_specs=pl.BlockSpec((1, tq, hg * HEAD_W), lambda bi, h, i: (bi, i, h)),
        out_shape=jax.ShapeDtypeStruct((b, sq, HEADS * HEAD_W), BF),
        scratch_shapes=[pltpu.VMEM((hg, 8, n_maps * tq), F32), pltpu.VMEM((hg, 8, n_maps * tq), F32),
                        pltpu.VMEM((hg, HEAD_W, n_maps * tq), F32)],
        compiler_params=_params(("parallel", "parallel", "arbitrary")),
        name="flash",
    )(*ins)


def _post_kernel(*refs, n_parts, final):
    x_ref = refs[0]
    part_refs = refs[1:1 + n_parts]
    wout_refs = refs[1 + n_parts:1 + 2 * n_parts]
    gffn_ref, wg_ref, wu_ref, wd_ref = refs[1 + 2 * n_parts:5 + 2 * n_parts]
    gfin_ref = refs[5 + 2 * n_parts] if final else None
    o_ref = refs[-1]
    x = x_ref[...]
    for p in range(n_parts):
        x = x + _dot(part_refs[p][...], wout_refs[p][...])
    h = _rms(x, gffn_ref[...], NORM_EPS).astype(BF)
    o_ref[...] = x
    for c in range(D_FF // FF_CHUNK):
        sl = slice(c * FF_CHUNK, (c + 1) * FF_CHUNK)
        g = _dot(h, wg_ref[:, sl])
        u = _dot(h, wu_ref[:, sl])
        a = (g * (1.0 / (1.0 + jnp.exp(-g))) * u).astype(BF)
        o_ref[...] += _dot(a, wd_ref[sl, :])
    if final:
        o_ref[...] = _rms(o_ref[...], gfin_ref[...], NORM_EPS)


def _post(x2d, parts, w_out, g_ffn, w_gate, w_up, w_down, g_final, tm):
    t = x2d.shape[0]
    final = g_final is not None
    row = lambda w: pl.BlockSpec((tm, w), lambda i: (i, 0))
    w_out = w_out.astype(BF)
    consts = [w_out[p * 1024:(p + 1) * 1024] for p in range(len(parts))]
    consts += [g_ffn.reshape(1, -1), w_gate.astype(BF), w_up.astype(BF), w_down.astype(BF)]
    if final:
        consts.append(g_final.reshape(1, -1))
    return pl.pallas_call(
        functools.partial(_post_kernel, n_parts=len(parts), final=final),
        grid=(t // tm,),
        in_specs=[row(D_MODEL)] + [row(1024) for _ in parts] + [_const_spec(c.shape) for c in consts],
        out_specs=row(D_MODEL),
        out_shape=jax.ShapeDtypeStruct((t, D_MODEL), F32),
        compiler_params=_params(("parallel",)),
        name="post",
    )(x2d, *parts, *consts)


def _odd_proj_kernel(x_ref, g_ref, wm_ref, wki_ref, ww_ref,
                     q_ref, k_ref, v_ref, k16_ref, v16_ref, qi_ref, ki_ref, ki16_ref, w_out_ref):
    h = _rms(x_ref[...], g_ref[...], NORM_EPS).astype(BF)
    q_ref[...] = (_dot(h, wm_ref[:, 0:1024]) * DSA_SCALE).astype(BF)
    k = _dot(h, wm_ref[:, 1024:2048])
    k_ref[...] = k
    k16_ref[...] = k.astype(BF)
    v = _dot(h, wm_ref[:, 2048:3072])
    v_ref[...] = v
    v16_ref[...] = v.astype(BF)
    qi_ref[...] = _dot(h, wm_ref[:, 3072:3584]).astype(BF)
    ki = _dot(h, wki_ref[...])
    ki_ref[...] = ki
    ki16_ref[...] = ki.astype(BF)
    w_out_ref[...] = _dot(h, ww_ref[...]) * IDX_SCALE


def _odd_proj(x2d, tm, g_attn, w_in):
    t = x2d.shape[0]
    c0 = 3 * 1024 + HEADS * IDX_DIM
    wm = w_in[:, :c0].astype(BF)
    wki = w_in[:, c0:c0 + IDX_DIM].astype(BF)
    ww = jnp.pad(w_in[:, c0 + IDX_DIM:], ((0, 0), (0, LANES - HEADS))).astype(BF)
    row = lambda w: pl.BlockSpec((tm, w), lambda i: (i, 0))
    out_shapes = [
        jax.ShapeDtypeStruct((t, 1024), BF),
        jax.ShapeDtypeStruct((t, 1024), F32),
        jax.ShapeDtypeStruct((t, 1024), F32),
        jax.ShapeDtypeStruct((t, 1024), BF),
        jax.ShapeDtypeStruct((t, 1024), BF),
        jax.ShapeDtypeStruct((t, HEADS * IDX_DIM), BF),
        jax.ShapeDtypeStruct((t, IDX_DIM), F32),
        jax.ShapeDtypeStruct((t, IDX_DIM), BF),
        jax.ShapeDtypeStruct((t, LANES), F32),
    ]
    return pl.pallas_call(
        _odd_proj_kernel,
        grid=(t // tm,),
        in_specs=[row(D_MODEL), _const_spec((1, D_MODEL)), _const_spec(wm.shape),
                  _const_spec(wki.shape), _const_spec(ww.shape)],
        out_specs=[row(s.shape[1]) for s in out_shapes],
        out_shape=out_shapes,
        compiler_params=_params(("parallel",)),
        name="odd_proj",
    )(x2d, g_attn.reshape(1, -1), wm, wki, ww)


def _indexer_kernel(qi_ref, wt_ref, k_ref, sel_ref, key_ref, cut_ref, *, tq, tk, past, top_k, nk,
                    idx_bits):
    qblk = pl.program_id(1)
    nkv = past // tk + qblk + 1
    q = qi_ref[0]
    wt = wt_ref[0]
    qh = [q[:, h * IDX_DIM:(h + 1) * IDX_DIM] for h in range(HEADS)]
    krow = lax.broadcasted_iota(jnp.int32, (tk, tq), 0)
    qcol = lax.broadcasted_iota(jnp.int32, (tk, tq), 1)
    kf = float(top_k)

    def keys_of(j, diagonal):
        k = k_ref[0, pl.ds(pl.multiple_of(j * tk, tk), tk), :]
        dots = [_dot_nt(k, qh[h]) for h in range(HEADS)]
        sc = jnp.zeros((tk, tq), F32)
        for h in range(HEADS):
            sc = sc + wt[h:h + 1, :] * jnp.maximum(dots[h], 0.0)
        sc = jnp.where(sc == 0.0, 0.0, sc)
        bits = pltpu.bitcast(sc, jnp.int32)
        key = jnp.where(bits < 0, bits ^ 0x7FFFFFFF, bits)
        if diagonal:
            key = jnp.where((krow // CHUNK) <= (qcol // CHUNK), key, INT_MIN)
        return key

    def score_body(j, carry):
        key_ref[j] = keys_of(j, False)
        return carry

    lax.fori_loop(0, nkv - 1, score_body, 0)
    key_ref[nkv - 1] = keys_of(nkv - 1, True)

    def count(pred):
        def body(j, acc):
            hit = jnp.where(pred(key_ref[j], j * tk + krow), 1.0, 0.0)
            for r in range(tk // ACC_ROWS):
                acc = acc + hit[r * ACC_ROWS:(r + 1) * ACC_ROWS]
            return acc
        acc = lax.fori_loop(0, nkv, body, jnp.zeros((ACC_ROWS, tq), F32))
        return jnp.sum(acc, axis=0, keepdims=True)

    c0 = count(lambda key, idx: key >= 0)
    thr = jnp.where(c0 >= kf, 0, INT_MIN).astype(jnp.int32)

    def bit_body(i, thr):
        cand = thr + lax.shift_left(jnp.int32(1), 30 - i)
        c = count(lambda key, idx: key >= cand)
        return jnp.where(c >= kf, cand, thr)

    thr = lax.fori_loop(0, 31, bit_body, thr)

    c_gt = count(lambda key, idx: key > thr)
    c_ge = count(lambda key, idx: key >= thr)
    need = kf - c_gt
    real = thr > KEY_NEG_INF
    excess = real & (c_ge - c_gt > need)
    any_excess = jnp.max(jnp.where(excess, 1.0, 0.0)) > 0.0

    def emit(keep_fn):
        def body(j, carry):
            keep = keep_fn(key_ref[j], j * tk + krow)
            sel_ref[0, 0, j] = jnp.where(keep, 0.0, -jnp.inf).astype(BF)
            return carry
        lax.fori_loop(0, nkv, body, 0)

    @pl.when(jnp.logical_not(any_excess))
    def _():
        thr1 = jnp.maximum(thr, KEY_NEG_INF + 1)
        emit(lambda key, idx: key >= thr1)

    @pl.when(any_excess)
    def _():
        def idx_body(i, cut):
            cand = cut + lax.shift_left(jnp.int32(1), idx_bits - 1 - i)
            c = count(lambda key, idx: (key == thr) & (idx < cand))
            return jnp.where(c < need, cand, cut)
        cut_ref[...] = lax.fori_loop(0, idx_bits, idx_body, jnp.zeros((1, tq), jnp.int32))
        cut = cut_ref[...]
        floor = jnp.maximum(thr, KEY_NEG_INF)
        emit(lambda key, idx: (key > floor) | ((key == thr) & real & (idx <= cut)))

    def zero_body(j, carry):
        sel_ref[0, 0, j] = jnp.zeros((tk, tq), BF)
        return carry

    lax.fori_loop(nkv, nk, zero_body, 0)


def _indexer(qidx, widx, kidx16, *, tq, past, top_k):
    b, sq, _ = qidx.shape
    sk = kidx16.shape[1]
    tk = tq
    nk = sk // tk
    idx_bits = max(1, (sk - 1).bit_length())
    kern = functools.partial(_indexer_kernel, tq=tq, tk=tk, past=past, top_k=top_k, nk=nk,
                             idx_bits=idx_bits)
    widx_t = widx[:, :, :HEADS].transpose(0, 2, 1)
    return pl.pallas_call(
        kern,
        grid=(b, sq // tq),
        in_specs=[pl.BlockSpec((1, tq, HEADS * IDX_DIM), lambda bi, i: (bi, i, 0)),
                  pl.BlockSpec((1, HEADS, tq), lambda bi, i: (bi, 0, i)),
                  pl.BlockSpec((1, sk, IDX_DIM), lambda bi, i: (bi, 0, 0))],
        out_specs=pl.BlockSpec((1, 1, nk, tk, tq), lambda bi, i: (bi, i, 0, 0, 0)),
        out_shape=jax.ShapeDtypeStruct((b, sq // tq, nk, tk, tq), BF),
        scratch_shapes=[pltpu.VMEM((nk, tk, tq), jnp.int32), pltpu.VMEM((1, tq), jnp.int32)],
        compiler_params=_params(("parallel", "arbitrary")),
        name="indexer",
    )(qidx, widx_t, kidx16)


def _rope_tables(pos):
    half = MLA_ROPE // 2
    inv = ROPE_THETA ** (-jnp.arange(half, dtype=F32) / half)
    ang = pos.astype(F32)[:, None] * inv[None, :]
    pad = jnp.zeros((pos.shape[0], HEAD_W - MLA_ROPE), F32)
    cos, sin = jnp.cos(ang), jnp.sin(ang)
    return (jnp.concatenate([cos, cos, pad], axis=1), jnp.concatenate([sin, sin, pad], axis=1))


def _row_tile(seq):
    return ATT_TILE if seq % ATT_TILE == 0 else CHUNK


def _pad_seq(a):
    n = -a.shape[1] % ATT_TILE
    return a if n == 0 else jnp.pad(a, ((0, 0), (0, n), (0, 0)))


def _even_layer(x, past_len, past, bias, p, lam_init):
    b, s, _ = x.shape
    tm = _row_tile(s)
    pos = past_len + jnp.arange(s, dtype=jnp.int32)
    cos_p, sin_p = _rope_tables(pos)
    x2d = x.reshape(b * s, D_MODEL)
    qcat, ckv, kpe, dq, dk, dv, dk16, dv16 = _even_proj(
        x2d, s, tm, p["g_attn"], p["w_in"], p["g_q"], p["w_q_up"], p["g_kv"], cos_p, sin_p)
    r3 = lambda a: a.reshape(b, s, -1)
    if past is None:
        ckv_all, kpe_all, dk_all, dv_all = r3(ckv), r3(kpe), r3(dk16), r3(dv16)
    else:
        c_ckv, c_kpe, c_dk, c_dv = past
        ckv_all = jnp.concatenate([c_ckv, r3(ckv)], axis=1)
        kpe_all = jnp.concatenate([c_kpe, r3(kpe)], axis=1)
        dk_all = jnp.concatenate([c_dk.reshape(b, past_len, -1).astype(BF), r3(dk16)], axis=1)
        dv_all = jnp.concatenate([c_dv.reshape(b, past_len, -1).astype(BF), r3(dv16)], axis=1)
    ckv_all, kpe_all, dk_all, dv_all = map(_pad_seq, (ckv_all, kpe_all, dk_all, dv_all))
    sk = ckv_all.shape[1]
    kcat, vmla = _kv_up(ckv_all.reshape(b * sk, KV_LORA), kpe_all.reshape(b * sk, MLA_ROPE),
                        p["w_kv_up"], ATT_TILE)
    a_out = _flash(_pad_seq(r3(qcat)), kcat.reshape(b, sk, -1), vmla.reshape(b, sk, -1),
                   tq=ATT_TILE, past=past_len, dq=MLA_QK)
    lam_params = jnp.stack([p["lq1"], p["lk1"], p["lq2"], p["lk2"]]).astype(F32)
    b_out = _flash(_pad_seq(r3(dq)), dk_all, dv_all, tq=ATT_TILE, past=past_len, dq=HEAD_W, bias=bias,
                   lam_params=lam_params, g_subln=p["g_subln"], lam_init=lam_init)
    parts = [a_out[:, :s].reshape(b * s, -1), b_out[:, :s].reshape(b * s, -1)]
    new = (r3(ckv), r3(kpe), dk.reshape(b, s, HEADS, 2 * DIFF_DIM), dv.reshape(b, s, HEADS, 2 * DIFF_DIM))
    return x2d, parts, new


def _odd_layer(x, past_len, past, bias, p):
    b, s, _ = x.shape
    tm = _row_tile(s)
    x2d = x.reshape(b * s, D_MODEL)
    q, k, v, k16, v16, qi, ki, ki16, widx = _odd_proj(x2d, tm, p["g_attn"], p["w_in"])
    r3 = lambda a: a.reshape(b, s, -1)
    if past is None:
        k_all, v_all, ki_all = r3(k16), r3(v16), r3(ki16)
    else:
        c_k, c_v, c_ki = past
        k_all = jnp.concatenate([c_k.reshape(b, past_len, -1).astype(BF), r3(k16)], axis=1)
        v_all = jnp.concatenate([c_v.reshape(b, past_len, -1).astype(BF), r3(v16)], axis=1)
        ki_all = jnp.concatenate([c_ki.astype(BF), r3(ki16)], axis=1)
    top_k = min(DSA_TOPK, k_all.shape[1] // 4)
    k_all, v_all, ki_all = map(_pad_seq, (k_all, v_all, ki_all))
    sel = _indexer(_pad_seq(r3(qi)), _pad_seq(r3(widx)), ki_all, tq=ATT_TILE, past=past_len, top_k=top_k)
    out = _flash(_pad_seq(r3(q)), k_all, v_all, tq=ATT_TILE, past=past_len, dq=HEAD_W, bias=bias, sel=sel)
    new = (k.reshape(b, s, HEADS, DSA_DIM), v.reshape(b, s, HEADS, DSA_DIM), r3(ki))
    return x2d, [out[:, :s].reshape(b * s, -1)], new


def kernel(x_prompt, x_sample, cache_mla_ckv, cache_mla_kpe, cache_diff_k, cache_diff_v, cache_dsa_k, cache_dsa_v, cache_dsa_kidx, g_attn_even, w_in_even, g_q_lora, w_q_up, g_kv_lora, w_kv_up, lambda_q1, lambda_k1, lambda_q2, lambda_k2, g_diff_subln, w_out_even, g_attn_odd, w_in_odd, w_out_odd, rel_bias_table, g_ffn, w_gate, w_up, w_down, g_final):
    depth = g_ffn.shape[0]
    past_len = cache_mla_ckv.shape[2]
    groups = [
        dict(x=x_prompt, past_len=0, has_past=False),
        dict(x=x_sample, past_len=past_len, has_past=True),
    ]
    bias = _bias_tiles(rel_bias_table, ATT_TILE, ATT_TILE)
    results = []
    for grp in groups:
        x = grp["x"]
        b, s, _ = x.shape
        tm = _row_tile(s)
        even_new, odd_new = [], []
        for layer in range(depth):
            if layer % 2 == 0:
                e = layer // 2
                lam_init = 0.8 - 0.6 * math.exp(-0.3 * layer)
                p = dict(g_attn=g_attn_even[e], w_in=w_in_even[e], g_q=g_q_lora[e], w_q_up=w_q_up[e],
                         g_kv=g_kv_lora[e], w_kv_up=w_kv_up[e], lq1=lambda_q1[e], lk1=lambda_k1[e],
                         lq2=lambda_q2[e], lk2=lambda_k2[e], g_subln=g_diff_subln[e])
                past = ((cache_mla_ckv[e], cache_mla_kpe[e], cache_diff_k[e], cache_diff_v[e])
                        if grp["has_past"] else None)
                x2d, parts, new = _even_layer(x, grp["past_len"], past, bias, p, lam_init)
                even_new.append(new)
                w_out = w_out_even[e]
            else:
                o = layer // 2
                p = dict(g_attn=g_attn_odd[o], w_in=w_in_odd[o])
                past = ((cache_dsa_k[o], cache_dsa_v[o], cache_dsa_kidx[o])
                        if grp["has_past"] else None)
                x2d, parts, new = _odd_layer(x, grp["past_len"], past, bias, p)
                odd_new.append(new)
                w_out = w_out_odd[o]
            gfin = g_final if layer == depth - 1 else None
            x = _post(x2d, parts, w_out, g_ffn[layer], w_gate[layer], w_up[layer], w_down[layer],
                      gfin, tm).reshape(b, s, D_MODEL)
        results.append((x, [jnp.stack(a) for a in zip(*even_new)], [jnp.stack(a) for a in zip(*odd_new)]))
    (yp, ep, op), (ys, es, os_) = results
    return (yp, ys, *ep, *op, *es, *os_)
```

```python
import functools
import math

import jax
import jax.numpy as jnp
from jax import lax
from jax.experimental import pallas as pl
from jax.experimental.pallas import tpu as pltpu

D_MODEL = 1024
CHUNK = 64
NORM_EPS = 1e-6
SUBLN_EPS = 1e-5
HEADS = 8
MLA_NOPE = 128
MLA_ROPE = 64
MLA_V = 128
Q_LORA = 384
KV_LORA = 256
ROPE_THETA = 10000.0
LOG2E = math.log2(math.e)
MLA_SCALE = (MLA_NOPE + MLA_ROPE) ** -0.5 * LOG2E
DIFF_DIM = 64
DIFF_SCALE = DIFF_DIM ** -0.5 * LOG2E
DSA_DIM = 128
DSA_SCALE = DSA_DIM ** -0.5 * LOG2E
IDX_DIM = 64
IDX_SCALE = IDX_DIM ** -0.5 * HEADS ** -0.5
DSA_TOPK = 256
NUM_BUCKETS = 32
T5_FAR = 91
HEAD_W = 128
MLA_QK = 256
D_FF = -(-8 * D_MODEL // (3 * 256)) * 256
FF_CHUNK = 256
HEADS_PER_STEP = 8
ATT_TILE = 256
ROW_TILE = 512
ONES_ROWS = 16

LANES = 128
VMEM_LIMIT = 56 * 1024 * 1024

INT_MIN = -2 ** 31
KEY_NEG_INF = (0xFF800000 ^ 0x7FFFFFFF) - 2 ** 32
ACC_ROWS = 32
NEG_BIG = -1e30
BF = jnp.bfloat16
F32 = jnp.float32


def _dot(a, b):
    return jnp.dot(a, b, preferred_element_type=F32)


def _dot_nt(a, b):
    return lax.dot_general(a, b, (((1,), (1,)), ((), ())), preferred_element_type=F32)


def _rms(x, g, eps):
    return x * lax.rsqrt(jnp.mean(x * x, axis=-1, keepdims=True) + eps) * g


def _params(sem):
    return pltpu.CompilerParams(dimension_semantics=sem, vmem_limit_bytes=VMEM_LIMIT)


def _const_spec(shape):
    nd = len(shape)
    return pl.BlockSpec(shape, lambda *_: (0,) * nd, pipeline_mode=pl.Buffered(1))


def _even_proj_kernel(x_ref, g_ref, wq_ref, wkv_ref, wd_ref, wkpe_ref, wkrot_ref, gq_ref,
                      wqcat_ref, wqrot_ref, gkv_ref, cos_ref, sin_ref,
                      qcat_ref, ckv_ref, kpe_ref, dq_ref, dk_ref, dv_ref, dk16_ref, dv16_ref):
    h = _rms(x_ref[...], g_ref[...], NORM_EPS).astype(BF)
    cos = cos_ref[...]
    sin = sin_ref[...]
    qn = _rms(_dot(h, wq_ref[...]), gq_ref[...], NORM_EPS).astype(BF)
    qc = _dot(qn, wqcat_ref[...])
    qr = _dot(qn, wqrot_ref[...])
    for hh in range(HEADS):
        a = hh * MLA_QK
        qcat_ref[:, a:a + HEAD_W] = (qc[:, a:a + HEAD_W] * MLA_SCALE).astype(BF)
        pe = qc[:, a + HEAD_W:a + MLA_QK] * cos + qr[:, hh * HEAD_W:(hh + 1) * HEAD_W] * sin
        qcat_ref[:, a + HEAD_W:a + MLA_QK] = (pe * MLA_SCALE).astype(BF)
    ckv_ref[...] = _rms(_dot(h, wkv_ref[...]), gkv_ref[...], NORM_EPS)
    kpe_ref[...] = (_dot(h, wkpe_ref[...]) * cos[:, :MLA_ROPE]
                    + _dot(h, wkrot_ref[...]) * sin[:, :MLA_ROPE])
    dq_ref[...] = (_dot(h, wd_ref[:, 0:1024]) * DIFF_SCALE).astype(BF)
    dk = _dot(h, wd_ref[:, 1024:2048])
    dk_ref[...] = dk
    dk16_ref[...] = dk.astype(BF)
    dv = _dot(h, wd_ref[:, 2048:3072])
    dv_ref[...] = dv
    dv16_ref[...] = dv.astype(BF)


def _even_proj(x2d, seq, tm, g_attn, w_in, g_q, w_q_up, g_kv, cos_p, sin_p):
    t = x2d.shape[0]
    c0, c1, c2 = Q_LORA, Q_LORA + KV_LORA, Q_LORA + KV_LORA + MLA_ROPE
    half = MLA_ROPE // 2
    wq = w_in[:, :c0].astype(BF)
    wkv = w_in[:, c0:c1].astype(BF)
    wkpe = w_in[:, c1:c2]
    wkrot = jnp.concatenate([-wkpe[:, half:], wkpe[:, :half]], axis=1).astype(BF)
    wkpe = wkpe.astype(BF)
    wd = w_in[:, c2:].astype(BF)
    wqu = w_q_up.reshape(Q_LORA, HEADS, MLA_NOPE + MLA_ROPE)
    w_nope, w_pe = wqu[..., :MLA_NOPE], wqu[..., MLA_NOPE:]
    w_rot = jnp.concatenate([-w_pe[..., half:], w_pe[..., :half]], axis=-1)
    zpad = jnp.zeros((Q_LORA, HEADS, MLA_QK - MLA_NOPE - MLA_ROPE), F32)
    wqcat = jnp.concatenate([w_nope, w_pe, zpad], axis=-1).reshape(Q_LORA, HEADS * MLA_QK).astype(BF)
    wqrot = jnp.concatenate([w_rot, zpad], axis=-1).reshape(Q_LORA, HEADS * HEAD_W).astype(BF)
    n_pos = seq // tm
    row = lambda w: pl.BlockSpec((tm, w), lambda i: (i, 0))
    pos = pl.BlockSpec((tm, HEAD_W), lambda i: (i % n_pos, 0))
    out_shapes = [
        jax.ShapeDtypeStruct((t, HEADS * MLA_QK), BF),
        jax.ShapeDtypeStruct((t, KV_LORA), F32),
        jax.ShapeDtypeStruct((t, MLA_ROPE), F32),
        jax.ShapeDtypeStruct((t, 1024), BF),
        jax.ShapeDtypeStruct((t, 1024), F32),
        jax.ShapeDtypeStruct((t, 1024), F32),
        jax.ShapeDtypeStruct((t, 1024), BF),
        jax.ShapeDtypeStruct((t, 1024), BF),
    ]
    ins = [x2d, g_attn.reshape(1, -1), wq, wkv, wd, wkpe, wkrot, g_q.reshape(1, -1), wqcat, wqrot,
           g_kv.reshape(1, -1), cos_p, sin_p]
    in_specs = [row(D_MODEL)] + [_const_spec(a.shape) for a in ins[1:11]] + [pos, pos]
    return pl.pallas_call(
        _even_proj_kernel,
        grid=(t // tm,),
        in_specs=in_specs,
        out_specs=[row(s.shape[1]) for s in out_shapes],
        out_shape=out_shapes,
        compiler_params=_params(("parallel",)),
        name="even_proj",
    )(*ins)


def _kv_up_kernel(ckv_ref, kpe_ref, wk_ref, wv_ref, kcat_ref, v_ref):
    c = ckv_ref[...].astype(BF)
    kn = _dot(c, wk_ref[...])
    v_ref[...] = _dot(c, wv_ref[...]).astype(BF)
    kpe = kpe_ref[...]
    kp = jnp.concatenate([kpe, jnp.zeros_like(kpe)], axis=-1).astype(BF)
    for hh in range(HEADS):
        a = hh * MLA_QK
        kcat_ref[:, a:a + HEAD_W] = kn[:, hh * HEAD_W:(hh + 1) * HEAD_W].astype(BF)
        kcat_ref[:, a + HEAD_W:a + MLA_QK] = kp


def _kv_up(ckv2d, kpe2d, w_kv_up, tm):
    t = ckv2d.shape[0]
    wkv = w_kv_up.reshape(KV_LORA, HEADS, MLA_NOPE + MLA_V)
    wk = wkv[..., :MLA_NOPE].reshape(KV_LORA, HEADS * MLA_NOPE).astype(BF)
    wv = wkv[..., MLA_NOPE:].reshape(KV_LORA, HEADS * MLA_V).astype(BF)
    row = lambda w: pl.BlockSpec((tm, w), lambda i: (i, 0))
    return pl.pallas_call(
        _kv_up_kernel,
        grid=(t // tm,),
        in_specs=[row(KV_LORA), row(MLA_ROPE), _const_spec(wk.shape), _const_spec(wv.shape)],
        out_specs=[row(HEADS * MLA_QK), row(HEADS * MLA_V)],
        out_shape=[jax.ShapeDtypeStruct((t, HEADS * MLA_QK), BF),
                   jax.ShapeDtypeStruct((t, HEADS * MLA_V), BF)],
        compiler_params=_params(("parallel",)),
        name="kv_up",
    )(ckv2d, kpe2d, wk, wv)


def _n_near(tk):
    return 1 + -(-(T5_FAR - 1) // tk)


def _bias_kernel(tab_ref, out_ref, *, tq, tk, n_near):
    h = pl.program_id(0)
    col = lax.broadcasted_iota(jnp.int32, (tk, tq), 0)
    row = lax.broadcasted_iota(jnp.int32, (tk, tq), 1)
    far = tab_ref[NUM_BUCKETS // 2 - 1, h]
    for n in range(1, n_near + 1):
        rel = col - row - (n_near - n) * tk
        dist = jnp.abs(rel)
        large = jnp.full((tk, tq), NUM_BUCKETS // 4, jnp.int32)
        for thr in (12, 16, 23, 32, 46, 64, T5_FAR):
            large = large + jnp.where(dist >= thr, 1, 0)
        bucket = jnp.where(rel > 0, NUM_BUCKETS // 2, 0) + jnp.where(dist < NUM_BUCKETS // 4, dist, large)
        val = jnp.zeros((tk, tq), F32)
        for b in range(NUM_BUCKETS):
            val = jnp.where(bucket == b, tab_ref[b, h], val)
        val = (val - far) * LOG2E
        if n == n_near:
            val = jnp.where((col // CHUNK) <= (row // CHUNK), val, -jnp.inf)
        out_ref[0, n - 1] = val


def _bias_tiles(rel_table, tq, tk):
    n_near = _n_near(tk)
    return pl.pallas_call(
        functools.partial(_bias_kernel, tq=tq, tk=tk, n_near=n_near),
        grid=(HEADS,),
        in_specs=[pl.BlockSpec(memory_space=pltpu.SMEM)],
        out_specs=pl.BlockSpec((1, n_near, tk, tq), lambda h: (h, 0, 0, 0)),
        out_shape=jax.ShapeDtypeStruct((HEADS, n_near, tk, tq), F32),
        compiler_params=_params(("arbitrary",)),
        name="bias_tiles",
    )(rel_table)


def _flash_kernel(*refs, tq, tk, past, n_maps, has_bias, has_sel, lam_init, hg, dq):
    refs = list(refs)
    q_ref, k_ref, vt_ref = refs[:3]
    pos = 3
    bias_ref = sel_ref = lam_ref = gsub_ref = None
    if has_bias:
        bias_ref = refs[pos]; pos += 1
    if has_sel:
        sel_ref = refs[pos]; pos += 1
    if n_maps == 2:
        lam_ref, gsub_ref = refs[pos], refs[pos + 1]; pos += 2
    o_ref, m_ref, l_ref, acc_ref = refs[pos:pos + 4]
    n_near = _n_near(tk) if has_bias else 1
    ntq = n_maps * tq

    j_diag = past // tk + pl.program_id(2)
    qs = []
    for h in range(hg):
        q = q_ref[0, :, h * dq:(h + 1) * dq]
        if n_maps == 2:
            lane = lax.broadcasted_iota(jnp.int32, q.shape, 1)
            q = jnp.concatenate([jnp.where(lane < DIFF_DIM, q, jnp.zeros_like(q)),
                                 jnp.where(lane >= DIFF_DIM, q, jnp.zeros_like(q))], axis=0)
        qs.append(q)
    m_ref[...] = jnp.full(m_ref.shape, NEG_BIG, F32)
    l_ref[...] = jnp.zeros(l_ref.shape, F32)
    acc_ref[...] = jnp.zeros(acc_ref.shape, F32)
    ones = jnp.ones((ONES_ROWS, tk), BF)

    def tile(j, near):
        start = pl.multiple_of(j * tk, tk)
        extra = None
        if not has_bias and near is not None:
            krow = lax.broadcasted_iota(jnp.int32, (tk, tq), 0)
            qcol = lax.broadcasted_iota(jnp.int32, (tk, tq), 1)
            extra = jnp.where((krow // CHUNK) <= (qcol // CHUNK), 0.0, -jnp.inf)
        if has_sel:
            extra = sel_ref[0, 0, j].astype(F32)
        ss = []
        for h in range(hg):
            k = k_ref[0, pl.ds(start, tk), h * dq:(h + 1) * dq]
            ss.append(_dot_nt(k, qs[h]))
        pp, aa = [], []
        for h in range(hg):
            add = extra
            if has_bias and near is not None:
                add = bias_ref[h, near] if add is None else add + bias_ref[h, near]
            s = ss[h]
            if add is not None:
                s = s + (add if n_maps == 1 else jnp.concatenate([add] * n_maps, axis=1))
            m_old = m_ref[h]
            m_new = jnp.maximum(m_old, jnp.max(s, axis=0, keepdims=True))
            aa.append(jnp.exp2(m_old - m_new))
            pp.append(jnp.exp2(s - m_new[:1]).astype(BF))
            m_ref[h] = m_new
        for h in range(hg):
            v1t = jnp.concatenate([vt_ref[j, h * HEAD_W:(h + 1) * HEAD_W, :], ones], axis=0)
            pv = _dot(v1t, pp[h])
            l_ref[h] = aa[h] * l_ref[h] + pv[HEAD_W:HEAD_W + 8]
            acc_ref[h] = aa[h][:1] * acc_ref[h] + pv[:HEAD_W]

    def far_body(j, carry):
        tile(j, None)
        return carry

    lax.fori_loop(0, jnp.maximum(j_diag - (n_near - 1), 0), far_body, 0)
    for n in range(n_near):
        j = j_diag - (n_near - 1) + n
        if n == n_near - 1:
            tile(j, n)
        else:
            pl.when(j >= 0)(functools.partial(tile, j, n))

    if n_maps == 2:
        lp = lam_ref[...]
        lam = (jnp.exp(jnp.sum(lp[0:1] * lp[1:2], axis=-1, keepdims=True))
               - jnp.exp(jnp.sum(lp[2:3] * lp[3:4], axis=-1, keepdims=True)) + lam_init)
    for h in range(hg):
        out = acc_ref[h] * (1.0 / l_ref[h][:1])
        if n_maps == 2:
            out = out[:, :tq] - lam * out[:, tq:]
            ms = jnp.mean(out * out, axis=0, keepdims=True)
            out = out * lax.rsqrt(ms + SUBLN_EPS) * gsub_ref[...] * (1.0 - lam_init)
        o_ref[0, :, h * HEAD_W:(h + 1) * HEAD_W] = out.T.astype(o_ref.dtype)


def _flash(q, k, v, *, tq, past, dq, bias=None, sel=None, lam_params=None, g_subln=None,
           lam_init=0.0):
    b, sq, _ = q.shape
    sk = k.shape[1]
    tk = tq
    nk = sk // tk
    n_maps = 2 if lam_params is not None else 1
    hg = HEADS_PER_STEP
    vt = v.reshape(b, nk, tk, HEADS * HEAD_W).transpose(0, 1, 3, 2).reshape(b * nk, HEADS * HEAD_W, tk)
    ins = [q, k, vt]
    once = pl.Buffered(1)
    in_specs = [
        pl.BlockSpec((1, tq, hg * dq), lambda bi, h, i: (bi, i, h)),
        pl.BlockSpec((1, sk, hg * dq), lambda bi, h, i: (bi, 0, h), pipeline_mode=once),
        pl.BlockSpec((nk, hg * HEAD_W, tk), lambda bi, h, i: (bi, h, 0), pipeline_mode=once),
    ]
    if bias is not None:
        ins.append(bias)
        in_specs.append(pl.BlockSpec((hg,) + bias.shape[1:], lambda bi, h, i: (h, 0, 0, 0),
                                     pipeline_mode=once))
    if sel is not None:
        ins.append(sel)
        in_specs.append(pl.BlockSpec((1, 1) + sel.shape[2:], lambda bi, h, i: (bi, i, 0, 0, 0)))
    if n_maps == 2:
        ins += [lam_params, jnp.broadcast_to(g_subln.astype(F32)[:, None], (HEAD_W, tq))]
        in_specs += [pl.BlockSpec(lam_params.shape, lambda bi, h, i: (0, 0)),
                     pl.BlockSpec((HEAD_W, tq), lambda bi, h, i: (0, 0))]
    kern = functools.partial(_flash_kernel, tq=tq, tk=tk, past=past, n_maps=n_maps,
                             has_bias=bias is not None, has_sel=sel is not None,
                             lam_init=lam_init, hg=hg, dq=dq)
    return pl.pallas_call(
        kern,
        grid=(b, HEADS // hg, sq // tq),
        in_specs=in_specs,
        out_specs=pl.BlockSpec((1, tq, hg * HEAD_W), lambda bi, h, i: (bi, i, h)),
        out_shape=jax.ShapeDtypeStruct((b, sq, HEADS * HEAD_W), BF),
        scratch_shapes=[pltpu.VMEM((hg, 8, n_maps * tq), F32), pltpu.VMEM((hg, 8, n_maps * tq), F32),
                        pltpu.VMEM((hg, HEAD_W, n_maps * tq), F32)],
        compiler_params=_params(("parallel", "parallel", "arbitrary")),
        name="flash",
    )(*ins)


def _post_kernel(*refs, n_parts, final):
    x_ref = refs[0]
    part_refs = refs[1:1 + n_parts]
    wout_refs = refs[1 + n_parts:1 + 2 * n_parts]
    gffn_ref, wg_ref, wu_ref, wd_ref = refs[1 + 2 * n_parts:5 + 2 * n_parts]
    gfin_ref = refs[5 + 2 * n_parts] if final else None
    o_ref, act_ref = refs[-2:]
    x = x_ref[...]
    for p in range(n_parts):
        x = x + _dot(part_refs[p][...], wout_refs[p][...])
    h = _rms(x, gffn_ref[...], NORM_EPS).astype(BF)
    for c in range(D_FF // FF_CHUNK):
        sl = slice(c * FF_CHUNK, (c + 1) * FF_CHUNK)
        g = _dot(h, wg_ref[:, sl])
        u = _dot(h, wu_ref[:, sl])
        act_ref[:, sl] = (g * (1.0 / (1.0 + jnp.exp(-g))) * u).astype(BF)
    y = x + _dot(act_ref[...], wd_ref[...])
    if final:
        y = _rms(y, gfin_ref[...], NORM_EPS)
    o_ref[...] = y


def _post(x2d, parts, w_out, g_ffn, w_gate, w_up, w_down, g_final, tm):
    t = x2d.shape[0]
    final = g_final is not None
    row = lambda w: pl.BlockSpec((tm, w), lambda i: (i, 0))
    w_out = w_out.astype(BF)
    consts = [w_out[p * 1024:(p + 1) * 1024] for p in range(len(parts))]
    consts += [g_ffn.reshape(1, -1), w_gate.astype(BF), w_up.astype(BF), w_down.astype(BF)]
    if final:
        consts.append(g_final.reshape(1, -1))
    return pl.pallas_call(
        functools.partial(_post_kernel, n_parts=len(parts), final=final),
        grid=(t // tm,),
        in_specs=[row(D_MODEL)] + [row(1024) for _ in parts] + [_const_spec(c.shape) for c in consts],
        out_specs=row(D_MODEL),
        out_shape=jax.ShapeDtypeStruct((t, D_MODEL), F32),
        scratch_shapes=[pltpu.VMEM((tm, D_FF), BF)],
        compiler_params=_params(("parallel",)),
        name="post",
    )(x2d, *parts, *consts)


def _odd_proj_kernel(x_ref, g_ref, wm_ref, wki_ref, ww_ref,
                     q_ref, k_ref, v_ref, k16_ref, v16_ref, qi_ref, ki_ref, ki16_ref, w_out_ref):
    h = _rms(x_ref[...], g_ref[...], NORM_EPS).astype(BF)
    q_ref[...] = (_dot(h, wm_ref[:, 0:1024]) * DSA_SCALE).astype(BF)
    k = _dot(h, wm_ref[:, 1024:2048])
    k_ref[...] = k
    k16_ref[...] = k.astype(BF)
    v = _dot(h, wm_ref[:, 2048:3072])
    v_ref[...] = v
    v16_ref[...] = v.astype(BF)
    qi_ref[...] = _dot(h, wm_ref[:, 3072:3584]).astype(BF)
    ki = _dot(h, wki_ref[...])
    ki_ref[...] = ki
    ki16_ref[...] = ki.astype(BF)
    w_out_ref[...] = _dot(h, ww_ref[...]) * IDX_SCALE


def _odd_proj(x2d, tm, g_attn, w_in):
    t = x2d.shape[0]
    c0 = 3 * 1024 + HEADS * IDX_DIM
    wm = w_in[:, :c0].astype(BF)
    wki = w_in[:, c0:c0 + IDX_DIM].astype(BF)
    ww = jnp.pad(w_in[:, c0 + IDX_DIM:], ((0, 0), (0, LANES - HEADS))).astype(BF)
    row = lambda w: pl.BlockSpec((tm, w), lambda i: (i, 0))
    out_shapes = [
        jax.ShapeDtypeStruct((t, 1024), BF),
        jax.ShapeDtypeStruct((t, 1024), F32),
        jax.ShapeDtypeStruct((t, 1024), F32),
        jax.ShapeDtypeStruct((t, 1024), BF),
        jax.ShapeDtypeStruct((t, 1024), BF),
        jax.ShapeDtypeStruct((t, HEADS * IDX_DIM), BF),
        jax.ShapeDtypeStruct((t, IDX_DIM), F32),
        jax.ShapeDtypeStruct((t, IDX_DIM), BF),
        jax.ShapeDtypeStruct((t, LANES), F32),
    ]
    return pl.pallas_call(
        _odd_proj_kernel,
        grid=(t // tm,),
        in_specs=[row(D_MODEL), _const_spec((1, D_MODEL)), _const_spec(wm.shape),
                  _const_spec(wki.shape), _const_spec(ww.shape)],
        out_specs=[row(s.shape[1]) for s in out_shapes],
        out_shape=out_shapes,
        compiler_params=_params(("parallel",)),
        name="odd_proj",
    )(x2d, g_attn.reshape(1, -1), wm, wki, ww)


def _indexer_kernel(qi_ref, wt_ref, k_ref, sel_ref, key_ref, cut_ref, *, tq, tk, past, top_k, nk,
                    idx_bits):
    qblk = pl.program_id(1)
    nkv = past // tk + qblk + 1
    q = qi_ref[0]
    wt = wt_ref[0]
    qh = [q[:, h * IDX_DIM:(h + 1) * IDX_DIM] for h in range(HEADS)]
    krow = lax.broadcasted_iota(jnp.int32, (tk, tq), 0)
    qcol = lax.broadcasted_iota(jnp.int32, (tk, tq), 1)
    kf = float(top_k)

    def keys_of(j, diagonal):
        k = k_ref[0, pl.ds(pl.multiple_of(j * tk, tk), tk), :]
        dots = [_dot_nt(k, qh[h]) for h in range(HEADS)]
        sc = jnp.zeros((tk, tq), F32)
        for h in range(HEADS):
            sc = sc + wt[h:h + 1, :] * jnp.maximum(dots[h], 0.0)
        sc = jnp.where(sc == 0.0, 0.0, sc)
        bits = pltpu.bitcast(sc, jnp.int32)
        key = jnp.where(bits < 0, bits ^ 0x7FFFFFFF, bits)
        if diagonal:
            key = jnp.where((krow // CHUNK) <= (qcol // CHUNK), key, INT_MIN)
        return key

    def score_pair(i, carry):
        key_ref[2 * i] = keys_of(2 * i, False)
        key_ref[2 * i + 1] = keys_of(2 * i + 1, False)
        return carry

    n_far = nkv - 1
    lax.fori_loop(0, n_far // 2, score_pair, 0)

    @pl.when(n_far % 2 == 1)
    def _():
        key_ref[n_far - 1] = keys_of(n_far - 1, False)

    key_ref[nkv - 1] = keys_of(nkv - 1, True)

    def count(pred):
        def body(j, acc):
            hit = jnp.where(pred(key_ref[j], j * tk + krow), 1.0, 0.0)
            for r in range(tk // ACC_ROWS):
                acc = acc + hit[r * ACC_ROWS:(r + 1) * ACC_ROWS]
            return acc
        acc = lax.fori_loop(0, nkv, body, jnp.zeros((ACC_ROWS, tq), F32))
        return jnp.sum(acc, axis=0, keepdims=True)

    c0 = count(lambda key, idx: key >= 0)
    thr = jnp.where(c0 >= kf, 0, INT_MIN).astype(jnp.int32)

    def bit_body(i, thr):
        cand = thr + lax.shift_left(jnp.int32(1), 30 - i)
        c = count(lambda key, idx: key >= cand)
        return jnp.where(c >= kf, cand, thr)

    thr = lax.fori_loop(0, 31, bit_body, thr)

    c_gt = count(lambda key, idx: key > thr)
    c_ge = count(lambda key, idx: key >= thr)
    need = kf - c_gt
    real = thr > KEY_NEG_INF
    excess = real & (c_ge - c_gt > need)
    any_excess = jnp.max(jnp.where(excess, 1.0, 0.0)) > 0.0

    def emit(keep_fn):
        def body(j, carry):
            keep = keep_fn(key_ref[j], j * tk + krow)
            sel_ref[0, 0, j] = jnp.where(keep, 0.0, -jnp.inf).astype(BF)
            return carry
        lax.fori_loop(0, nkv, body, 0)

    @pl.when(jnp.logical_not(any_excess))
    def _():
        thr1 = jnp.maximum(thr, KEY_NEG_INF + 1)
        emit(lambda key, idx: key >= thr1)

    @pl.when(any_excess)
    def _():
        def idx_body(i, cut):
            cand = cut + lax.shift_left(jnp.int32(1), idx_bits - 1 - i)
            c = count(lambda key, idx: (key == thr) & (idx < cand))
            return jnp.where(c < need, cand, cut)
        cut_ref[...] = lax.fori_loop(0, idx_bits, idx_body, jnp.zeros((1, tq), jnp.int32))
        cut = cut_ref[...]
        floor = jnp.maximum(thr, KEY_NEG_INF)
        emit(lambda key, idx: (key > floor) | ((key == thr) & real & (idx <= cut)))

    def zero_body(j, carry):
        sel_ref[0, 0, j] = jnp.zeros((tk, tq), BF)
        return carry

    lax.fori_loop(nkv, nk, zero_body, 0)


def _indexer(qidx, widx, kidx16, *, tq, past, top_k):
    b, sq, _ = qidx.shape
    sk = kidx16.shape[1]
    tk = tq
    nk = sk // tk
    idx_bits = max(1, (sk - 1).bit_length())
    kern = functools.partial(_indexer_kernel, tq=tq, tk=tk, past=past, top_k=top_k, nk=nk,
                             idx_bits=idx_bits)
    widx_t = widx[:, :, :HEADS].transpose(0, 2, 1)
    return pl.pallas_call(
        kern,
        grid=(b, sq // tq),
        in_specs=[pl.BlockSpec((1, tq, HEADS * IDX_DIM), lambda bi, i: (bi, i, 0)),
                  pl.BlockSpec((1, HEADS, tq), lambda bi, i: (bi, 0, i)),
                  pl.BlockSpec((1, sk, IDX_DIM), lambda bi, i: (bi, 0, 0))],
        out_specs=pl.BlockSpec((1, 1, nk, tk, tq), lambda bi, i: (bi, i, 0, 0, 0)),
        out_shape=jax.ShapeDtypeStruct((b, sq // tq, nk, tk, tq), BF),
        scratch_shapes=[pltpu.VMEM((nk, tk, tq), jnp.int32), pltpu.VMEM((1, tq), jnp.int32)],
        compiler_params=_params(("parallel", "arbitrary")),
        name="indexer",
    )(qidx, widx_t, kidx16)


def _rope_tables(pos):
    half = MLA_ROPE // 2
    inv = ROPE_THETA ** (-jnp.arange(half, dtype=F32) / half)
    ang = pos.astype(F32)[:, None] * inv[None, :]
    pad = jnp.zeros((pos.shape[0], HEAD_W - MLA_ROPE), F32)
    cos, sin = jnp.cos(ang), jnp.sin(ang)
    return (jnp.concatenate([cos, cos, pad], axis=1), jnp.concatenate([sin, sin, pad], axis=1))


def _row_tile(seq):
    return ROW_TILE if seq % ROW_TILE == 0 else CHUNK


def _pad_seq(a):
    n = -a.shape[1] % ATT_TILE
    return a if n == 0 else jnp.pad(a, ((0, 0), (0, n), (0, 0)))


def _even_layer(x, past_len, past, bias, p, lam_init):
    b, s, _ = x.shape
    tm = _row_tile(s)
    pos = past_len + jnp.arange(s, dtype=jnp.int32)
    cos_p, sin_p = _rope_tables(pos)
    x2d = x.reshape(b * s, D_MODEL)
    qcat, ckv, kpe, dq, dk, dv, dk16, dv16 = _even_proj(
        x2d, s, tm, p["g_attn"], p["w_in"], p["g_q"], p["w_q_up"], p["g_kv"], cos_p, sin_p)
    r3 = lambda a: a.reshape(b, s, -1)
    if past is None:
        ckv_all, kpe_all, dk_all, dv_all = r3(ckv), r3(kpe), r3(dk16), r3(dv16)
    else:
        c_ckv, c_kpe, c_dk, c_dv = past
        ckv_all = jnp.concatenate([c_ckv, r3(ckv)], axis=1)
        kpe_all = jnp.concatenate([c_kpe, r3(kpe)], axis=1)
        dk_all = jnp.concatenate([c_dk.reshape(b, past_len, -1).astype(BF), r3(dk16)], axis=1)
        dv_all = jnp.concatenate([c_dv.reshape(b, past_len, -1).astype(BF), r3(dv16)], axis=1)
    ckv_all, kpe_all, dk_all, dv_all = map(_pad_seq, (ckv_all, kpe_all, dk_all, dv_all))
    sk = ckv_all.shape[1]
    kcat, vmla = _kv_up(ckv_all.reshape(b * sk, KV_LORA), kpe_all.reshape(b * sk, MLA_ROPE),
                        p["w_kv_up"], ATT_TILE)
    a_out = _flash(_pad_seq(r3(qcat)), kcat.reshape(b, sk, -1), vmla.reshape(b, sk, -1),
                   tq=ATT_TILE, past=past_len, dq=MLA_QK)
    lam_params = jnp.stack([p["lq1"], p["lk1"], p["lq2"], p["lk2"]]).astype(F32)
    b_out = _flash(_pad_seq(r3(dq)), dk_all, dv_all, tq=ATT_TILE, past=past_len, dq=HEAD_W, bias=bias,
                   lam_params=lam_params, g_subln=p["g_subln"], lam_init=lam_init)
    parts = [a_out[:, :s].reshape(b * s, -1), b_out[:, :s].reshape(b * s, -1)]
    new = (r3(ckv), r3(kpe), dk.reshape(b, s, HEADS, 2 * DIFF_DIM), dv.reshape(b, s, HEADS, 2 * DIFF_DIM))
    return x2d, parts, new


def _odd_layer(x, past_len, past, bias, p):
    b, s, _ = x.shape
    tm = _row_tile(s)
    x2d = x.reshape(b * s, D_MODEL)
    q, k, v, k16, v16, qi, ki, ki16, widx = _odd_proj(x2d, tm, p["g_attn"], p["w_in"])
    r3 = lambda a: a.reshape(b, s, -1)
    if past is None:
        k_all, v_all, ki_all = r3(k16), r3(v16), r3(ki16)
    else:
        c_k, c_v, c_ki = past
        k_all = jnp.concatenate([c_k.reshape(b, past_len, -1).astype(BF), r3(k16)], axis=1)
        v_all = jnp.concatenate([c_v.reshape(b, past_len, -1).astype(BF), r3(v16)], axis=1)
        ki_all = jnp.concatenate([c_ki.astype(BF), r3(ki16)], axis=1)
    top_k = min(DSA_TOPK, k_all.shape[1] // 4)
    k_all, v_all, ki_all = map(_pad_seq, (k_all, v_all, ki_all))
    sel = _indexer(_pad_seq(r3(qi)), _pad_seq(r3(widx)), ki_all, tq=ATT_TILE, past=past_len, top_k=top_k)
    out = _flash(_pad_seq(r3(q)), k_all, v_all, tq=ATT_TILE, past=past_len, dq=HEAD_W, bias=bias, sel=sel)
    new = (k.reshape(b, s, HEADS, DSA_DIM), v.reshape(b, s, HEADS, DSA_DIM), r3(ki))
    return x2d, [out[:, :s].reshape(b * s, -1)], new


def kernel(x_prompt, x_sample, cache_mla_ckv, cache_mla_kpe, cache_diff_k, cache_diff_v, cache_dsa_k, cache_dsa_v, cache_dsa_kidx, g_attn_even, w_in_even, g_q_lora, w_q_up, g_kv_lora, w_kv_up, lambda_q1, lambda_k1, lambda_q2, lambda_k2, g_diff_subln, w_out_even, g_attn_odd, w_in_odd, w_out_odd, rel_bias_table, g_ffn, w_gate, w_up, w_down, g_final):
    depth = g_ffn.shape[0]
    past_len = cache_mla_ckv.shape[2]
    groups = [
        dict(x=x_prompt, past_len=0, has_past=False),
        dict(x=x_sample, past_len=past_len, has_past=True),
    ]
    bias = _bias_tiles(rel_bias_table, ATT_TILE, ATT_TILE)
    results = []
    for grp in groups:
        x = grp["x"]
        b, s, _ = x.shape
        tm = _row_tile(s)
        even_new, odd_new = [], []
        for layer in range(depth):
            if layer % 2 == 0:
                e = layer // 2
                lam_init = 0.8 - 0.6 * math.exp(-0.3 * layer)
                p = dict(g_attn=g_attn_even[e], w_in=w_in_even[e], g_q=g_q_lora[e], w_q_up=w_q_up[e],
                         g_kv=g_kv_lora[e], w_kv_up=w_kv_up[e], lq1=lambda_q1[e], lk1=lambda_k1[e],
                         lq2=lambda_q2[e], lk2=lambda_k2[e], g_subln=g_diff_subln[e])
                past = ((cache_mla_ckv[e], cache_mla_kpe[e], cache_diff_k[e], cache_diff_v[e])
                        if grp["has_past"] else None)
                x2d, parts, new = _even_layer(x, grp["past_len"], past, bias, p, lam_init)
                even_new.append(new)
                w_out = w_out_even[e]
            else:
                o = layer // 2
                p = dict(g_attn=g_attn_odd[o], w_in=w_in_odd[o])
                past = ((cache_dsa_k[o], cache_dsa_v[o], cache_dsa_kidx[o])
                        if grp["has_past"] else None)
                x2d, parts, new = _odd_layer(x, grp["past_len"], past, bias, p)
                odd_new.append(new)
                w_out = w_out_odd[o]
            gfin = g_final if layer == depth - 1 else None
            x = _post(x2d, parts, w_out, g_ffn[layer], w_gate[layer], w_up[layer], w_down[layer],
                      gfin, tm).reshape(b, s, D_MODEL)
        results.append((x, [jnp.stack(a) for a in zip(*even_new)], [jnp.stack(a) for a in zip(*odd_new)]))
    (yp, ep, op), (ys, es, os_) = results
    return (yp, ys, *ep, *op, *es, *os_)
```

```python
import functools
import math

import jax
import jax.numpy as jnp
from jax import lax
from jax.experimental import pallas as pl
from jax.experimental.pallas import tpu as pltpu

D_MODEL = 1024
CHUNK = 64
NORM_EPS = 1e-6
SUBLN_EPS = 1e-5
HEADS = 8
MLA_NOPE = 128
MLA_ROPE = 64
MLA_V = 128
Q_LORA = 384
KV_LORA = 256
ROPE_THETA = 10000.0
LOG2E = math.log2(math.e)
MLA_SCALE = (MLA_NOPE + MLA_ROPE) ** -0.5 * LOG2E
DIFF_DIM = 64
DIFF_SCALE = DIFF_DIM ** -0.5 * LOG2E
DSA_DIM = 128
DSA_SCALE = DSA_DIM ** -0.5 * LOG2E
IDX_DIM = 64
IDX_SCALE = IDX_DIM ** -0.5 * HEADS ** -0.5
DSA_TOPK = 256
NUM_BUCKETS = 32
T5_FAR = 91
HEAD_W = 128
MLA_QK = 256
D_FF = -(-8 * D_MODEL // (3 * 256)) * 256
FF_CHUNK = 256
HEADS_PER_STEP = 8
ATT_TILE = 256
HEAD_LOOKAHEAD = {1: 8, 2: 2}
FAR_GROUP = 2
ROW_TILE = 512
ONES_ROWS = 16

LANES = 128
VMEM_LIMIT = 56 * 1024 * 1024

INT_MIN = -2 ** 31
KEY_NEG_INF = (0xFF800000 ^ 0x7FFFFFFF) - 2 ** 32
ACC_ROWS = 32
NEG_BIG = -1e30
BF = jnp.bfloat16
F32 = jnp.float32


def _dot(a, b):
    return jnp.dot(a, b, preferred_element_type=F32)


def _dot_nt(a, b):
    return lax.dot_general(a, b, (((1,), (1,)), ((), ())), preferred_element_type=F32)


def _rms(x, g, eps):
    return x * lax.rsqrt(jnp.mean(x * x, axis=-1, keepdims=True) + eps) * g


def _params(sem):
    return pltpu.CompilerParams(dimension_semantics=sem, vmem_limit_bytes=VMEM_LIMIT)


def _const_spec(shape):
    nd = len(shape)
    return pl.BlockSpec(shape, lambda *_: (0,) * nd, pipeline_mode=pl.Buffered(1))


def _even_proj_kernel(x_ref, g_ref, wq_ref, wkv_ref, wd_ref, wkpe_ref, wkrot_ref, gq_ref,
                      wqcat_ref, wqrot_ref, gkv_ref, cos_ref, sin_ref,
                      qcat_ref, ckv_ref, kpe_ref, dq_ref, dk_ref, dv_ref, dk16_ref, dv16_ref):
    h = _rms(x_ref[...], g_ref[...], NORM_EPS).astype(BF)
    cos = cos_ref[...]
    sin = sin_ref[...]
    qn = _rms(_dot(h, wq_ref[...]), gq_ref[...], NORM_EPS).astype(BF)
    qc = _dot(qn, wqcat_ref[...])
    qr = _dot(qn, wqrot_ref[...])
    for hh in range(HEADS):
        a = hh * MLA_QK
        qcat_ref[:, a:a + HEAD_W] = (qc[:, a:a + HEAD_W] * MLA_SCALE).astype(BF)
        pe = qc[:, a + HEAD_W:a + MLA_QK] * cos + qr[:, hh * HEAD_W:(hh + 1) * HEAD_W] * sin
        qcat_ref[:, a + HEAD_W:a + MLA_QK] = (pe * MLA_SCALE).astype(BF)
    ckv_ref[...] = _rms(_dot(h, wkv_ref[...]), gkv_ref[...], NORM_EPS)
    kpe_ref[...] = (_dot(h, wkpe_ref[...]) * cos[:, :MLA_ROPE]
                    + _dot(h, wkrot_ref[...]) * sin[:, :MLA_ROPE])
    dq_ref[...] = (_dot(h, wd_ref[:, 0:1024]) * DIFF_SCALE).astype(BF)
    dk = _dot(h, wd_ref[:, 1024:2048])
    dk_ref[...] = dk
    dk16_ref[...] = dk.astype(BF)
    dv = _dot(h, wd_ref[:, 2048:3072])
    dv_ref[...] = dv
    dv16_ref[...] = dv.astype(BF)


def _even_proj(x2d, seq, tm, g_attn, w_in, g_q, w_q_up, g_kv, cos_p, sin_p):
    t = x2d.shape[0]
    c0, c1, c2 = Q_LORA, Q_LORA + KV_LORA, Q_LORA + KV_LORA + MLA_ROPE
    half = MLA_ROPE // 2
    wq = w_in[:, :c0].astype(BF)
    wkv = w_in[:, c0:c1].astype(BF)
    wkpe = w_in[:, c1:c2]
    wkrot = jnp.concatenate([-wkpe[:, half:], wkpe[:, :half]], axis=1).astype(BF)
    wkpe = wkpe.astype(BF)
    wd = w_in[:, c2:].astype(BF)
    wqu = w_q_up.reshape(Q_LORA, HEADS, MLA_NOPE + MLA_ROPE)
    w_nope, w_pe = wqu[..., :MLA_NOPE], wqu[..., MLA_NOPE:]
    w_rot = jnp.concatenate([-w_pe[..., half:], w_pe[..., :half]], axis=-1)
    zpad = jnp.zeros((Q_LORA, HEADS, MLA_QK - MLA_NOPE - MLA_ROPE), F32)
    wqcat = jnp.concatenate([w_nope, w_pe, zpad], axis=-1).reshape(Q_LORA, HEADS * MLA_QK).astype(BF)
    wqrot = jnp.concatenate([w_rot, zpad], axis=-1).reshape(Q_LORA, HEADS * HEAD_W).astype(BF)
    n_pos = seq // tm
    row = lambda w: pl.BlockSpec((tm, w), lambda i: (i, 0))
    pos = pl.BlockSpec((tm, HEAD_W), lambda i: (i % n_pos, 0))
    out_shapes = [
        jax.ShapeDtypeStruct((t, HEADS * MLA_QK), BF),
        jax.ShapeDtypeStruct((t, KV_LORA), F32),
        jax.ShapeDtypeStruct((t, MLA_ROPE), F32),
        jax.ShapeDtypeStruct((t, 1024), BF),
        jax.ShapeDtypeStruct((t, 1024), F32),
        jax.ShapeDtypeStruct((t, 1024), F32),
        jax.ShapeDtypeStruct((t, 1024), BF),
        jax.ShapeDtypeStruct((t, 1024), BF),
    ]
    ins = [x2d, g_attn.reshape(1, -1), wq, wkv, wd, wkpe, wkrot, g_q.reshape(1, -1), wqcat, wqrot,
           g_kv.reshape(1, -1), cos_p, sin_p]
    in_specs = [row(D_MODEL)] + [_const_spec(a.shape) for a in ins[1:11]] + [pos, pos]
    return pl.pallas_call(
        _even_proj_kernel,
        grid=(t // tm,),
        in_specs=in_specs,
        out_specs=[row(s.shape[1]) for s in out_shapes],
        out_shape=out_shapes,
        compiler_params=_params(("parallel",)),
        name="even_proj",
    )(*ins)


def _kv_up_kernel(ckv_ref, kpe_ref, wk_ref, wv_ref, kcat_ref, v_ref):
    c = ckv_ref[...].astype(BF)
    kn = _dot(c, wk_ref[...])
    v_ref[...] = _dot(c, wv_ref[...]).astype(BF)
    kpe = kpe_ref[...]
    kp = jnp.concatenate([kpe, jnp.zeros_like(kpe)], axis=-1).astype(BF)
    for hh in range(HEADS):
        a = hh * MLA_QK
        kcat_ref[:, a:a + HEAD_W] = kn[:, hh * HEAD_W:(hh + 1) * HEAD_W].astype(BF)
        kcat_ref[:, a + HEAD_W:a + MLA_QK] = kp


def _kv_up(ckv2d, kpe2d, w_kv_up, tm):
    t = ckv2d.shape[0]
    wkv = w_kv_up.reshape(KV_LORA, HEADS, MLA_NOPE + MLA_V)
    wk = wkv[..., :MLA_NOPE].reshape(KV_LORA, HEADS * MLA_NOPE).astype(BF)
    wv = wkv[..., MLA_NOPE:].reshape(KV_LORA, HEADS * MLA_V).astype(BF)
    row = lambda w: pl.BlockSpec((tm, w), lambda i: (i, 0))
    return pl.pallas_call(
        _kv_up_kernel,
        grid=(t // tm,),
        in_specs=[row(KV_LORA), row(MLA_ROPE), _const_spec(wk.shape), _const_spec(wv.shape)],
        out_specs=[row(HEADS * MLA_QK), row(HEADS * MLA_V)],
        out_shape=[jax.ShapeDtypeStruct((t, HEADS * MLA_QK), BF),
                   jax.ShapeDtypeStruct((t, HEADS * MLA_V), BF)],
        compiler_params=_params(("parallel",)),
        name="kv_up",
    )(ckv2d, kpe2d, wk, wv)


def _n_near(tk):
    return 1 + -(-(T5_FAR - 1) // tk)


def _bias_kernel(tab_ref, out_ref, *, tq, tk, n_near):
    h = pl.program_id(0)
    col = lax.broadcasted_iota(jnp.int32, (tk, tq), 0)
    row = lax.broadcasted_iota(jnp.int32, (tk, tq), 1)
    far = tab_ref[NUM_BUCKETS // 2 - 1, h]
    for n in range(1, n_near + 1):
        rel = col - row - (n_near - n) * tk
        dist = jnp.abs(rel)
        large = jnp.full((tk, tq), NUM_BUCKETS // 4, jnp.int32)
        for thr in (12, 16, 23, 32, 46, 64, T5_FAR):
            large = large + jnp.where(dist >= thr, 1, 0)
        bucket = jnp.where(rel > 0, NUM_BUCKETS // 2, 0) + jnp.where(dist < NUM_BUCKETS // 4, dist, large)
        val = jnp.zeros((tk, tq), F32)
        for b in range(NUM_BUCKETS):
            val = jnp.where(bucket == b, tab_ref[b, h], val)
        val = (val - far) * LOG2E
        if n == n_near:
            val = jnp.where((col // CHUNK) <= (row // CHUNK), val, -jnp.inf)
        out_ref[0, n - 1] = val


def _bias_tiles(rel_table, tq, tk):
    n_near = _n_near(tk)
    return pl.pallas_call(
        functools.partial(_bias_kernel, tq=tq, tk=tk, n_near=n_near),
        grid=(HEADS,),
        in_specs=[pl.BlockSpec(memory_space=pltpu.SMEM)],
        out_specs=pl.BlockSpec((1, n_near, tk, tq), lambda h: (h, 0, 0, 0)),
        out_shape=jax.ShapeDtypeStruct((HEADS, n_near, tk, tq), F32),
        compiler_params=_params(("arbitrary",)),
        name="bias_tiles",
    )(rel_table)


def _flash_kernel(*refs, tq, tk, past, n_maps, has_bias, has_sel, lam_init, hg, dq):
    refs = list(refs)
    q_ref, k_ref, vt_ref = refs[:3]
    pos = 3
    bias_ref = sel_ref = lam_ref = gsub_ref = None
    if has_bias:
        bias_ref = refs[pos]; pos += 1
    if has_sel:
        sel_ref = refs[pos]; pos += 1
    if n_maps == 2:
        lam_ref, gsub_ref = refs[pos], refs[pos + 1]; pos += 2
    o_ref, m_ref, l_ref, acc_ref = refs[pos:pos + 4]
    n_near = _n_near(tk) if has_bias else 1
    ntq = n_maps * tq

    j_diag = past // tk + pl.program_id(2)
    qs = []
    for h in range(hg):
        q = q_ref[0, :, h * dq:(h + 1) * dq]
        if n_maps == 2:
            lane = lax.broadcasted_iota(jnp.int32, q.shape, 1)
            q = jnp.concatenate([jnp.where(lane < DIFF_DIM, q, jnp.zeros_like(q)),
                                 jnp.where(lane >= DIFF_DIM, q, jnp.zeros_like(q))], axis=0)
        qs.append(q)
    m_ref[...] = jnp.full(m_ref.shape, NEG_BIG, F32)
    l_ref[...] = jnp.zeros(l_ref.shape, F32)
    acc_ref[...] = jnp.zeros(acc_ref.shape, F32)
    def tile(j, near, nt=1):
        start = pl.multiple_of(j * tk, tk)
        ones = jnp.ones((ONES_ROWS, nt * tk), BF)
        extra = None
        if not has_bias and near is not None:
            krow = lax.broadcasted_iota(jnp.int32, (tk, tq), 0)
            qcol = lax.broadcasted_iota(jnp.int32, (tk, tq), 1)
            extra = jnp.where((krow // CHUNK) <= (qcol // CHUNK), 0.0, -jnp.inf)
        if has_sel:
            extra = jnp.concatenate([sel_ref[0, 0, j + t] for t in range(nt)], axis=0).astype(F32)
        ss, pp, aa = {}, {}, {}

        def logits(h):
            k = k_ref[0, pl.ds(start, nt * tk), h * dq:(h + 1) * dq]
            ss[h] = _dot_nt(k, qs[h])

        def softmax(h):
            add = extra
            if has_bias and near is not None:
                add = bias_ref[h, near] if add is None else add + bias_ref[h, near]
            s = ss.pop(h)
            if add is not None:
                s = s + (add if n_maps == 1 else jnp.concatenate([add] * n_maps, axis=1))
            m_old = m_ref[h]
            m_new = jnp.maximum(m_old, jnp.max(s, axis=0, keepdims=True))
            aa[h] = jnp.exp2(m_old - m_new)
            pp[h] = jnp.exp2(s - m_new[:1]).astype(BF)
            m_ref[h] = m_new

        def values(h):
            vt = [vt_ref[j + t, h * HEAD_W:(h + 1) * HEAD_W, :] for t in range(nt)]
            v1t = jnp.concatenate([vt[0] if nt == 1 else jnp.concatenate(vt, axis=1), ones], axis=0)
            pv = _dot(v1t, pp.pop(h))
            a = aa.pop(h)
            l_ref[h] = a * l_ref[h] + pv[HEAD_W:HEAD_W + 8]
            acc_ref[h] = a[:1] * acc_ref[h] + pv[:HEAD_W]

        ahead = min(HEAD_LOOKAHEAD[n_maps], hg)
        for h in range(ahead):
            logits(h)
        behind = 1 if ahead < hg else hg
        for h in range(hg):
            if h + ahead < hg:
                logits(h + ahead)
            softmax(h)
            if h >= behind:
                values(h - behind)
        for h in range(max(hg - behind, 0), hg):
            values(h)

    def far_pair(i, carry):
        tile(2 * i, None, FAR_GROUP)
        return carry

    n_far = jnp.maximum(j_diag - (n_near - 1), 0)
    lax.fori_loop(0, n_far // FAR_GROUP, far_pair, 0)
    pl.when(n_far % FAR_GROUP == 1)(functools.partial(tile, n_far - 1, None))
    for n in range(n_near):
        j = j_diag - (n_near - 1) + n
        if n == n_near - 1:
            tile(j, n)
        else:
            pl.when(j >= 0)(functools.partial(tile, j, n))

    if n_maps == 2:
        lp = lam_ref[...]
        lam = (jnp.exp(jnp.sum(lp[0:1] * lp[1:2], axis=-1, keepdims=True))
               - jnp.exp(jnp.sum(lp[2:3] * lp[3:4], axis=-1, keepdims=True)) + lam_init)
    for h in range(hg):
        out = acc_ref[h] * (1.0 / l_ref[h][:1])
        if n_maps == 2:
            out = out[:, :tq] - lam * out[:, tq:]
            ms = jnp.mean(out * out, axis=0, keepdims=True)
            out = out * lax.rsqrt(ms + SUBLN_EPS) * gsub_ref[...] * (1.0 - lam_init)
        o_ref[0, :, h * HEAD_W:(h + 1) * HEAD_W] = out.T.astype(o_ref.dtype)


def _flash(q, k, v, *, tq, past, dq, bias=None, sel=None, lam_params=None, g_subln=None,
           lam_init=0.0):
    b, sq, _ = q.shape
    sk = k.shape[1]
    tk = tq
    nk = sk // tk
    n_maps = 2 if lam_params is not None else 1
    hg = HEADS_PER_STEP
    vt = v.reshape(b, nk, tk, HEADS * HEAD_W).transpose(0, 1, 3, 2).reshape(b * nk, HEADS * HEAD_W, tk)
    ins = [q, k, vt]
    once = pl.Buffered(1)
    in_specs = [
        pl.BlockSpec((1, tq, hg * dq), lambda bi, h, i: (bi, i, h)),
        pl.BlockSpec((1, sk, hg * dq), lambda bi, h, i: (bi, 0, h), pipeline_mode=once),
        pl.BlockSpec((nk, hg * HEAD_W, tk), lambda bi, h, i: (bi, h, 0), pipeline_mode=once),
    ]
    if bias is not None:
        ins.append(bias)
        in_specs.append(pl.BlockSpec((hg,) + bias.shape[1:], lambda bi, h, i: (h, 0, 0, 0),
                                     pipeline_mode=once))
    if sel is not None:
        ins.append(sel)
        in_specs.append(pl.BlockSpec((1, 1) + sel.shape[2:], lambda bi, h, i: (bi, i, 0, 0, 0)))
    if n_maps == 2:
        ins += [lam_params, jnp.broadcast_to(g_subln.astype(F32)[:, None], (HEAD_W, tq))]
        in_specs += [pl.BlockSpec(lam_params.shape, lambda bi, h, i: (0, 0)),
                     pl.BlockSpec((HEAD_W, tq), lambda bi, h, i: (0, 0))]
    kern = functools.partial(_flash_kernel, tq=tq, tk=tk, past=past, n_maps=n_maps,
                             has_bias=bias is not None, has_sel=sel is not None,
                             lam_init=lam_init, hg=hg, dq=dq)
    return pl.pallas_call(
        kern,
        grid=(b, HEADS // hg, sq // tq),
        in_specs=in_specs,
        out_specs=pl.BlockSpec((1, tq, hg * HEAD_W), lambda bi, h, i: (bi, i, h)),
        out_shape=jax.ShapeDtypeStruct((b, sq, HEADS * HEAD_W), BF),
        scratch_shapes=[pltpu.VMEM((hg, 8, n_maps * tq), F32), pltpu.VMEM((hg, 8, n_maps * tq), F32),
                        pltpu.VMEM((hg, HEAD_W, n_maps * tq), F32)],
        compiler_params=_params(("parallel", "parallel", "arbitrary")),
        name="flash",
    )(*ins)


def _post_kernel(*refs, n_parts, final):
    x_ref = refs[0]
    part_refs = refs[1:1 + n_parts]
    wout_refs = refs[1 + n_parts:1 + 2 * n_parts]
    gffn_ref, wg_ref, wu_ref, wd_ref = refs[1 + 2 * n_parts:5 + 2 * n_parts]
    gfin_ref = refs[5 + 2 * n_parts] if final else None
    o_ref, act_ref = refs[-2:]
    x = x_ref[...]
    for p in range(n_parts):
        x = x + _dot(part_refs[p][...], wout_refs[p][...])
    h = _rms(x, gffn_ref[...], NORM_EPS).astype(BF)
    for c in range(D_FF // FF_CHUNK):
        sl = slice(c * FF_CHUNK, (c + 1) * FF_CHUNK)
        g = _dot(h, wg_ref[:, sl])
        u = _dot(h, wu_ref[:, sl])
        act_ref[:, sl] = (g * (1.0 / (1.0 + jnp.exp(-g))) * u).astype(BF)
    y = x + _dot(act_ref[...], wd_ref[...])
    if final:
        y = _rms(y, gfin_ref[...], NORM_EPS)
    o_ref[...] = y


def _post(x2d, parts, w_out, g_ffn, w_gate, w_up, w_down, g_final, tm):
    t = x2d.shape[0]
    final = g_final is not None
    row = lambda w: pl.BlockSpec((tm, w), lambda i: (i, 0))
    w_out = w_out.astype(BF)
    consts = [w_out[p * 1024:(p + 1) * 1024] for p in range(len(parts))]
    consts += [g_ffn.reshape(1, -1), w_gate.astype(BF), w_up.astype(BF), w_down.astype(BF)]
    if final:
        consts.append(g_final.reshape(1, -1))
    return pl.pallas_call(
        functools.partial(_post_kernel, n_parts=len(parts), final=final),
        grid=(t // tm,),
        in_specs=[row(D_MODEL)] + [row(1024) for _ in parts] + [_const_spec(c.shape) for c in consts],
        out_specs=row(D_MODEL),
        out_shape=jax.ShapeDtypeStruct((t, D_MODEL), F32),
        scratch_shapes=[pltpu.VMEM((tm, D_FF), BF)],
        compiler_params=_params(("parallel",)),
        name="post",
    )(x2d, *parts, *consts)


def _odd_proj_kernel(x_ref, g_ref, wm_ref, wki_ref, ww_ref,
                     q_ref, k_ref, v_ref, k16_ref, v16_ref, qi_ref, ki_ref, ki16_ref, w_out_ref):
    h = _rms(x_ref[...], g_ref[...], NORM_EPS).astype(BF)
    q_ref[...] = (_dot(h, wm_ref[:, 0:1024]) * DSA_SCALE).astype(BF)
    k = _dot(h, wm_ref[:, 1024:2048])
    k_ref[...] = k
    k16_ref[...] = k.astype(BF)
    v = _dot(h, wm_ref[:, 2048:3072])
    v_ref[...] = v
    v16_ref[...] = v.astype(BF)
    qi_ref[...] = _dot(h, wm_ref[:, 3072:3584]).astype(BF)
    ki = _dot(h, wki_ref[...])
    ki_ref[...] = ki
    ki16_ref[...] = ki.astype(BF)
    w_out_ref[...] = _dot(h, ww_ref[...]) * IDX_SCALE


def _odd_proj(x2d, tm, g_attn, w_in):
    t = x2d.shape[0]
    c0 = 3 * 1024 + HEADS * IDX_DIM
    wm = w_in[:, :c0].astype(BF)
    wki = w_in[:, c0:c0 + IDX_DIM].astype(BF)
    ww = jnp.pad(w_in[:, c0 + IDX_DIM:], ((0, 0), (0, LANES - HEADS))).astype(BF)
    row = lambda w: pl.BlockSpec((tm, w), lambda i: (i, 0))
    out_shapes = [
        jax.ShapeDtypeStruct((t, 1024), BF),
        jax.ShapeDtypeStruct((t, 1024), F32),
        jax.ShapeDtypeStruct((t, 1024), F32),
        jax.ShapeDtypeStruct((t, 1024), BF),
        jax.ShapeDtypeStruct((t, 1024), BF),
        jax.ShapeDtypeStruct((t, HEADS * IDX_DIM), BF),
        jax.ShapeDtypeStruct((t, IDX_DIM), F32),
        jax.ShapeDtypeStruct((t, IDX_DIM), BF),
        jax.ShapeDtypeStruct((t, LANES), F32),
    ]
    return pl.pallas_call(
        _odd_proj_kernel,
        grid=(t // tm,),
        in_specs=[row(D_MODEL), _const_spec((1, D_MODEL)), _const_spec(wm.shape),
                  _const_spec(wki.shape), _const_spec(ww.shape)],
        out_specs=[row(s.shape[1]) for s in out_shapes],
        out_shape=out_shapes,
        compiler_params=_params(("parallel",)),
        name="odd_proj",
    )(x2d, g_attn.reshape(1, -1), wm, wki, ww)


def _indexer_kernel(qi_ref, wt_ref, k_ref, sel_ref, key_ref, cut_ref, *, tq, tk, past, top_k, nk,
                    idx_bits):
    qblk = pl.program_id(1)
    nkv = past // tk + qblk + 1
    q = qi_ref[0]
    wt = wt_ref[0]
    qh = [q[:, h * IDX_DIM:(h + 1) * IDX_DIM] for h in range(HEADS)]
    krow = lax.broadcasted_iota(jnp.int32, (tk, tq), 0)
    qcol = lax.broadcasted_iota(jnp.int32, (tk, tq), 1)
    kf = float(top_k)

    def keys_of(j, diagonal):
        k = k_ref[0, pl.ds(pl.multiple_of(j * tk, tk), tk), :]
        dots = [_dot_nt(k, qh[h]) for h in range(HEADS)]
        sc = jnp.zeros((tk, tq), F32)
        for h in range(HEADS):
            sc = sc + wt[h:h + 1, :] * jnp.maximum(dots[h], 0.0)
        sc = jnp.where(sc == 0.0, 0.0, sc)
        bits = pltpu.bitcast(sc, jnp.int32)
        key = jnp.where(bits < 0, bits ^ 0x7FFFFFFF, bits)
        if diagonal:
            key = jnp.where((krow // CHUNK) <= (qcol // CHUNK), key, INT_MIN)
        return key

    def score_pair(i, carry):
        key_ref[2 * i] = keys_of(2 * i, False)
        key_ref[2 * i + 1] = keys_of(2 * i + 1, False)
        return carry

    n_far = nkv - 1
    lax.fori_loop(0, n_far // 2, score_pair, 0)

    @pl.when(n_far % 2 == 1)
    def _():
        key_ref[n_far - 1] = keys_of(n_far - 1, False)

    key_ref[nkv - 1] = keys_of(nkv - 1, True)

    def count(pred):
        def body(j, acc):
            hit = jnp.where(pred(key_ref[j], j * tk + krow), 1.0, 0.0)
            for r in range(tk // ACC_ROWS):
                acc = acc + hit[r * ACC_ROWS:(r + 1) * ACC_ROWS]
            return acc
        acc = lax.fori_loop(0, nkv, body, jnp.zeros((ACC_ROWS, tq), F32))
        return jnp.sum(acc, axis=0, keepdims=True)

    c0 = count(lambda key, idx: key >= 0)
    thr = jnp.where(c0 >= kf, 0, INT_MIN).astype(jnp.int32)

    def bit_body(i, thr):
        cand = thr + lax.shift_left(jnp.int32(1), 30 - i)
        c = count(lambda key, idx: key >= cand)
        return jnp.where(c >= kf, cand, thr)

    thr = lax.fori_loop(0, 31, bit_body, thr)

    c_gt = count(lambda key, idx: key > thr)
    c_ge = count(lambda key, idx: key >= thr)
    need = kf - c_gt
    real = thr > KEY_NEG_INF
    excess = real & (c_ge - c_gt > need)
    any_excess = jnp.max(jnp.where(excess, 1.0, 0.0)) > 0.0

    def emit(keep_fn):
        def body(j, carry):
            keep = keep_fn(key_ref[j], j * tk + krow)
            sel_ref[0, 0, j] = jnp.where(keep, 0.0, -jnp.inf).astype(BF)
            return carry
        lax.fori_loop(0, nkv, body, 0)

    @pl.when(jnp.logical_not(any_excess))
    def _():
        thr1 = jnp.maximum(thr, KEY_NEG_INF + 1)
        emit(lambda key, idx: key >= thr1)

    @pl.when(any_excess)
    def _():
        def idx_body(i, cut):
            cand = cut + lax.shift_left(jnp.int32(1), idx_bits - 1 - i)
            c = count(lambda key, idx: (key == thr) & (idx < cand))
            return jnp.where(c < need, cand, cut)
        cut_ref[...] = lax.fori_loop(0, idx_bits, idx_body, jnp.zeros((1, tq), jnp.int32))
        cut = cut_ref[...]
        floor = jnp.maximum(thr, KEY_NEG_INF)
        emit(lambda key, idx: (key > floor) | ((key == thr) & real & (idx <= cut)))

    def zero_body(j, carry):
        sel_ref[0, 0, j] = jnp.zeros((tk, tq), BF)
        return carry

    lax.fori_loop(nkv, nk, zero_body, 0)


def _indexer(qidx, widx, kidx16, *, tq, past, top_k):
    b, sq, _ = qidx.shape
    sk = kidx16.shape[1]
    tk = tq
    nk = sk // tk
    idx_bits = max(1, (sk - 1).bit_length())
    kern = functools.partial(_indexer_kernel, tq=tq, tk=tk, past=past, top_k=top_k, nk=nk,
                             idx_bits=idx_bits)
    widx_t = widx[:, :, :HEADS].transpose(0, 2, 1)
    return pl.pallas_call(
        kern,
        grid=(b, sq // tq),
        in_specs=[pl.BlockSpec((1, tq, HEADS * IDX_DIM), lambda bi, i: (bi, i, 0)),
                  pl.BlockSpec((1, HEADS, tq), lambda bi, i: (bi, 0, i)),
                  pl.BlockSpec((1, sk, IDX_DIM), lambda bi, i: (bi, 0, 0))],
        out_specs=pl.BlockSpec((1, 1, nk, tk, tq), lambda bi, i: (bi, i, 0, 0, 0)),
        out_shape=jax.ShapeDtypeStruct((b, sq // tq, nk, tk, tq), BF),
        scratch_shapes=[pltpu.VMEM((nk, tk, tq), jnp.int32), pltpu.VMEM((1, tq), jnp.int32)],
        compiler_params=_params(("parallel", "arbitrary")),
        name="indexer",
    )(qidx, widx_t, kidx16)


def _rope_tables(pos):
    half = MLA_ROPE // 2
    inv = ROPE_THETA ** (-jnp.arange(half, dtype=F32) / half)
    ang = pos.astype(F32)[:, None] * inv[None, :]
    pad = jnp.zeros((pos.shape[0], HEAD_W - MLA_ROPE), F32)
    cos, sin = jnp.cos(ang), jnp.sin(ang)
    return (jnp.concatenate([cos, cos, pad], axis=1), jnp.concatenate([sin, sin, pad], axis=1))


def _row_tile(seq):
    return ROW_TILE if seq % ROW_TILE == 0 else CHUNK


def _pad_seq(a):
    n = -a.shape[1] % ATT_TILE
    return a if n == 0 else jnp.pad(a, ((0, 0), (0, n), (0, 0)))


def _even_layer(x, past_len, past, bias, p, lam_init):
    b, s, _ = x.shape
    tm = _row_tile(s)
    pos = past_len + jnp.arange(s, dtype=jnp.int32)
    cos_p, sin_p = _rope_tables(pos)
    x2d = x.reshape(b * s, D_MODEL)
    qcat, ckv, kpe, dq, dk, dv, dk16, dv16 = _even_proj(
        x2d, s, tm, p["g_attn"], p["w_in"], p["g_q"], p["w_q_up"], p["g_kv"], cos_p, sin_p)
    r3 = lambda a: a.reshape(b, s, -1)
    if past is None:
        ckv_all, kpe_all, dk_all, dv_all = r3(ckv), r3(kpe), r3(dk16), r3(dv16)
    else:
        c_ckv, c_kpe, c_dk, c_dv = past
        ckv_all = jnp.concatenate([c_ckv, r3(ckv)], axis=1)
        kpe_all = jnp.concatenate([c_kpe, r3(kpe)], axis=1)
        dk_all = jnp.concatenate([c_dk.reshape(b, past_len, -1).astype(BF), r3(dk16)], axis=1)
        dv_all = jnp.concatenate([c_dv.reshape(b, past_len, -1).astype(BF), r3(dv16)], axis=1)
    ckv_all, kpe_all, dk_all, dv_all = map(_pad_seq, (ckv_all, kpe_all, dk_all, dv_all))
    sk = ckv_all.shape[1]
    kcat, vmla = _kv_up(ckv_all.reshape(b * sk, KV_LORA), kpe_all.reshape(b * sk, MLA_ROPE),
                        p["w_kv_up"], ROW_TILE if (b * sk) % ROW_TILE == 0 else ATT_TILE)
    a_out = _flash(_pad_seq(r3(qcat)), kcat.reshape(b, sk, -1), vmla.reshape(b, sk, -1),
                   tq=ATT_TILE, past=past_len, dq=MLA_QK)
    lam_params = jnp.stack([p["lq1"], p["lk1"], p["lq2"], p["lk2"]]).astype(F32)
    b_out = _flash(_pad_seq(r3(dq)), dk_all, dv_all, tq=ATT_TILE, past=past_len, dq=HEAD_W, bias=bias,
                   lam_params=lam_params, g_subln=p["g_subln"], lam_init=lam_init)
    parts = [a_out[:, :s].reshape(b * s, -1), b_out[:, :s].reshape(b * s, -1)]
    new = (r3(ckv), r3(kpe), dk.reshape(b, s, HEADS, 2 * DIFF_DIM), dv.reshape(b, s, HEADS, 2 * DIFF_DIM))
    return x2d, parts, new


def _odd_layer(x, past_len, past, bias, p):
    b, s, _ = x.shape
    tm = _row_tile(s)
    x2d = x.reshape(b * s, D_MODEL)
    q, k, v, k16, v16, qi, ki, ki16, widx = _odd_proj(x2d, tm, p["g_attn"], p["w_in"])
    r3 = lambda a: a.reshape(b, s, -1)
    if past is None:
        k_all, v_all, ki_all = r3(k16), r3(v16), r3(ki16)
    else:
        c_k, c_v, c_ki = past
        k_all = jnp.concatenate([c_k.reshape(b, past_len, -1).astype(BF), r3(k16)], axis=1)
        v_all = jnp.concatenate([c_v.reshape(b, past_len, -1).astype(BF), r3(v16)], axis=1)
        ki_all = jnp.concatenate([c_ki.astype(BF), r3(ki16)], axis=1)
    top_k = min(DSA_TOPK, k_all.shape[1] // 4)
    k_all, v_all, ki_all = map(_pad_seq, (k_all, v_all, ki_all))
    sel = _indexer(_pad_seq(r3(qi)), _pad_seq(r3(widx)), ki_all, tq=ATT_TILE, past=past_len, top_k=top_k)
    out = _flash(_pad_seq(r3(q)), k_all, v_all, tq=ATT_TILE, past=past_len, dq=HEAD_W, bias=bias, sel=sel)
    new = (k.reshape(b, s, HEADS, DSA_DIM), v.reshape(b, s, HEADS, DSA_DIM), r3(ki))
    return x2d, [out[:, :s].reshape(b * s, -1)], new


def kernel(x_prompt, x_sample, cache_mla_ckv, cache_mla_kpe, cache_diff_k, cache_diff_v, cache_dsa_k, cache_dsa_v, cache_dsa_kidx, g_attn_even, w_in_even, g_q_lora, w_q_up, g_kv_lora, w_kv_up, lambda_q1, lambda_k1, lambda_q2, lambda_k2, g_diff_subln, w_out_even, g_attn_odd, w_in_odd, w_out_odd, rel_bias_table, g_ffn, w_gate, w_up, w_down, g_final):
    depth = g_ffn.shape[0]
    past_len = cache_mla_ckv.shape[2]
    groups = [
        dict(x=x_prompt, past_len=0, has_past=False),
        dict(x=x_sample, past_len=past_len, has_past=True),
    ]
    bias = _bias_tiles(rel_bias_table, ATT_TILE, ATT_TILE)
    results = []
    for grp in groups:
        x = grp["x"]
        b, s, _ = x.shape
        tm = _row_tile(s)
        even_new, odd_new = [], []
        for layer in range(depth):
            if layer % 2 == 0:
                e = layer // 2
                lam_init = 0.8 - 0.6 * math.exp(-0.3 * layer)
                p = dict(g_attn=g_attn_even[e], w_in=w_in_even[e], g_q=g_q_lora[e], w_q_up=w_q_up[e],
                         g_kv=g_kv_lora[e], w_kv_up=w_kv_up[e], lq1=lambda_q1[e], lk1=lambda_k1[e],
                         lq2=lambda_q2[e], lk2=lambda_k2[e], g_subln=g_diff_subln[e])
                past = ((cache_mla_ckv[e], cache_mla_kpe[e], cache_diff_k[e], cache_diff_v[e])
                        if grp["has_past"] else None)
                x2d, parts, new = _even_layer(x, grp["past_len"], past, bias, p, lam_init)
                even_new.append(new)
                w_out = w_out_even[e]
            else:
                o = layer // 2
                p = dict(g_attn=g_attn_odd[o], w_in=w_in_odd[o])
                past = ((cache_dsa_k[o], cache_dsa_v[o], cache_dsa_kidx[o])
                        if grp["has_past"] else None)
                x2d, parts, new = _odd_layer(x, grp["past_len"], past, bias, p)
                odd_new.append(new)
                w_out = w_out_odd[o]
            gfin = g_final if layer == depth - 1 else None
            x = _post(x2d, parts, w_out, g_ffn[layer], w_gate[layer], w_up[layer], w_down[layer],
                      gfin, tm).reshape(b, s, D_MODEL)
        results.append((x, [jnp.stack(a) for a in zip(*even_new)], [jnp.stack(a) for a in zip(*odd_new)]))
    (yp, ep, op), (ys, es, os_) = results
    return (yp, ys, *ep, *op, *es, *os_)
```

```python
import functools
import math

import jax
import jax.numpy as jnp
from jax import lax
from jax.experimental import pallas as pl
from jax.experimental.pallas import tpu as pltpu

D_MODEL = 1024
CHUNK = 64
NORM_EPS = 1e-6
SUBLN_EPS = 1e-5
HEADS = 8
MLA_NOPE = 128
MLA_ROPE = 64
MLA_V = 128
Q_LORA = 384
KV_LORA = 256
ROPE_THETA = 10000.0
LOG2E = math.log2(math.e)
MLA_SCALE = (MLA_NOPE + MLA_ROPE) ** -0.5 * LOG2E
DIFF_DIM = 64
DIFF_SCALE = DIFF_DIM ** -0.5 * LOG2E
DSA_DIM = 128
DSA_SCALE = DSA_DIM ** -0.5 * LOG2E
IDX_DIM = 64
IDX_SCALE = IDX_DIM ** -0.5 * HEADS ** -0.5
DSA_TOPK = 256
NUM_BUCKETS = 32
T5_FAR = 91
HEAD_W = 128
MLA_QK = 256
D_FF = -(-8 * D_MODEL // (3 * 256)) * 256
FF_CHUNK = 256
HEADS_PER_STEP = 8
ATT_TILE = 256
HEAD_LOOKAHEAD = {1: 8, 2: 8}
FAR_GROUP = {1: 2, 2: 1}
ROW_TILE = 512
ONES_ROWS = 16

LANES = 128
VMEM_LIMIT = 56 * 1024 * 1024

INT_MIN = -2 ** 31
KEY_NEG_INF = (0xFF800000 ^ 0x7FFFFFFF) - 2 ** 32
ACC_ROWS = 32
NEG_BIG = -1e30
BF = jnp.bfloat16
F32 = jnp.float32


def _dot(a, b):
    return jnp.dot(a, b, preferred_element_type=F32)


def _dot_nt(a, b):
    return lax.dot_general(a, b, (((1,), (1,)), ((), ())), preferred_element_type=F32)


def _rms(x, g, eps):
    return x * lax.rsqrt(jnp.mean(x * x, axis=-1, keepdims=True) + eps) * g


def _params(sem):
    return pltpu.CompilerParams(dimension_semantics=sem, vmem_limit_bytes=VMEM_LIMIT)


def _const_spec(shape):
    nd = len(shape)
    return pl.BlockSpec(shape, lambda *_: (0,) * nd, pipeline_mode=pl.Buffered(1))


def _even_proj_kernel(x_ref, g_ref, wq_ref, wkv_ref, wd_ref, wkpe_ref, wkrot_ref, gq_ref,
                      wqcat_ref, wqrot_ref, gkv_ref, cos_ref, sin_ref,
                      qcat_ref, ckv_ref, kpe_ref, dq_ref, dk_ref, dv_ref, dk16_ref, dv16_ref):
    h = _rms(x_ref[...], g_ref[...], NORM_EPS).astype(BF)
    cos = cos_ref[...]
    sin = sin_ref[...]
    qn = _rms(_dot(h, wq_ref[...]), gq_ref[...], NORM_EPS).astype(BF)
    qc = _dot(qn, wqcat_ref[...])
    qr = _dot(qn, wqrot_ref[...])
    for hh in range(HEADS):
        a = hh * MLA_QK
        qcat_ref[:, a:a + HEAD_W] = (qc[:, a:a + HEAD_W] * MLA_SCALE).astype(BF)
        pe = qc[:, a + HEAD_W:a + MLA_QK] * cos + qr[:, hh * HEAD_W:(hh + 1) * HEAD_W] * sin
        qcat_ref[:, a + HEAD_W:a + MLA_QK] = (pe * MLA_SCALE).astype(BF)
    ckv_ref[...] = _rms(_dot(h, wkv_ref[...]), gkv_ref[...], NORM_EPS)
    kpe_ref[...] = (_dot(h, wkpe_ref[...]) * cos[:, :MLA_ROPE]
                    + _dot(h, wkrot_ref[...]) * sin[:, :MLA_ROPE])
    dq_ref[...] = (_dot(h, wd_ref[:, 0:1024]) * DIFF_SCALE).astype(BF)
    dk = _dot(h, wd_ref[:, 1024:2048])
    dk_ref[...] = dk
    dk16_ref[...] = dk.astype(BF)
    dv = _dot(h, wd_ref[:, 2048:3072])
    dv_ref[...] = dv
    dv16_ref[...] = dv.astype(BF)


def _even_proj(x2d, seq, tm, g_attn, w_in, g_q, w_q_up, g_kv, cos_p, sin_p):
    t = x2d.shape[0]
    c0, c1, c2 = Q_LORA, Q_LORA + KV_LORA, Q_LORA + KV_LORA + MLA_ROPE
    half = MLA_ROPE // 2
    wq = w_in[:, :c0].astype(BF)
    wkv = w_in[:, c0:c1].astype(BF)
    wkpe = w_in[:, c1:c2]
    wkrot = jnp.concatenate([-wkpe[:, half:], wkpe[:, :half]], axis=1).astype(BF)
    wkpe = wkpe.astype(BF)
    wd = w_in[:, c2:].astype(BF)
    wqu = w_q_up.reshape(Q_LORA, HEADS, MLA_NOPE + MLA_ROPE)
    w_nope, w_pe = wqu[..., :MLA_NOPE], wqu[..., MLA_NOPE:]
    w_rot = jnp.concatenate([-w_pe[..., half:], w_pe[..., :half]], axis=-1)
    zpad = jnp.zeros((Q_LORA, HEADS, MLA_QK - MLA_NOPE - MLA_ROPE), F32)
    wqcat = jnp.concatenate([w_nope, w_pe, zpad], axis=-1).reshape(Q_LORA, HEADS * MLA_QK).astype(BF)
    wqrot = jnp.concatenate([w_rot, zpad], axis=-1).reshape(Q_LORA, HEADS * HEAD_W).astype(BF)
    n_pos = seq // tm
    row = lambda w: pl.BlockSpec((tm, w), lambda i: (i, 0))
    pos = pl.BlockSpec((tm, HEAD_W), lambda i: (i % n_pos, 0))
    out_shapes = [
        jax.ShapeDtypeStruct((t, HEADS * MLA_QK), BF),
        jax.ShapeDtypeStruct((t, KV_LORA), F32),
        jax.ShapeDtypeStruct((t, MLA_ROPE), F32),
        jax.ShapeDtypeStruct((t, 1024), BF),
        jax.ShapeDtypeStruct((t, 1024), F32),
        jax.ShapeDtypeStruct((t, 1024), F32),
        jax.ShapeDtypeStruct((t, 1024), BF),
        jax.ShapeDtypeStruct((t, 1024), BF),
    ]
    ins = [x2d, g_attn.reshape(1, -1), wq, wkv, wd, wkpe, wkrot, g_q.reshape(1, -1), wqcat, wqrot,
           g_kv.reshape(1, -1), cos_p, sin_p]
    in_specs = [row(D_MODEL)] + [_const_spec(a.shape) for a in ins[1:11]] + [pos, pos]
    return pl.pallas_call(
        _even_proj_kernel,
        grid=(t // tm,),
        in_specs=in_specs,
        out_specs=[row(s.shape[1]) for s in out_shapes],
        out_shape=out_shapes,
        compiler_params=_params(("parallel",)),
        name="even_proj",
    )(*ins)


def _kv_up_kernel(ckv_ref, kpe_ref, wk_ref, wv_ref, kcat_ref, v_ref):
    c = ckv_ref[...].astype(BF)
    kn = _dot(c, wk_ref[...])
    v_ref[...] = _dot(c, wv_ref[...]).astype(BF)
    kpe = kpe_ref[...]
    kp = jnp.concatenate([kpe, jnp.zeros_like(kpe)], axis=-1).astype(BF)
    for hh in range(HEADS):
        a = hh * MLA_QK
        kcat_ref[:, a:a + HEAD_W] = kn[:, hh * HEAD_W:(hh + 1) * HEAD_W].astype(BF)
        kcat_ref[:, a + HEAD_W:a + MLA_QK] = kp


def _kv_up(ckv2d, kpe2d, w_kv_up, tm):
    t = ckv2d.shape[0]
    wkv = w_kv_up.reshape(KV_LORA, HEADS, MLA_NOPE + MLA_V)
    wk = wkv[..., :MLA_NOPE].reshape(KV_LORA, HEADS * MLA_NOPE).astype(BF)
    wv = wkv[..., MLA_NOPE:].reshape(KV_LORA, HEADS * MLA_V).astype(BF)
    row = lambda w: pl.BlockSpec((tm, w), lambda i: (i, 0))
    return pl.pallas_call(
        _kv_up_kernel,
        grid=(t // tm,),
        in_specs=[row(KV_LORA), row(MLA_ROPE), _const_spec(wk.shape), _const_spec(wv.shape)],
        out_specs=[row(HEADS * MLA_QK), row(HEADS * MLA_V)],
        out_shape=[jax.ShapeDtypeStruct((t, HEADS * MLA_QK), BF),
                   jax.ShapeDtypeStruct((t, HEADS * MLA_V), BF)],
        compiler_params=_params(("parallel",)),
        name="kv_up",
    )(ckv2d, kpe2d, wk, wv)


def _n_near(tk):
    return 1 + -(-(T5_FAR - 1) // tk)


def _bias_kernel(tab_ref, out_ref, *, tq, tk, n_near):
    h = pl.program_id(0)
    col = lax.broadcasted_iota(jnp.int32, (tk, tq), 0)
    row = lax.broadcasted_iota(jnp.int32, (tk, tq), 1)
    far = tab_ref[NUM_BUCKETS // 2 - 1, h]
    for n in range(1, n_near + 1):
        rel = col - row - (n_near - n) * tk
        dist = jnp.abs(rel)
        large = jnp.full((tk, tq), NUM_BUCKETS // 4, jnp.int32)
        for thr in (12, 16, 23, 32, 46, 64, T5_FAR):
            large = large + jnp.where(dist >= thr, 1, 0)
        bucket = jnp.where(rel > 0, NUM_BUCKETS // 2, 0) + jnp.where(dist < NUM_BUCKETS // 4, dist, large)
        val = jnp.zeros((tk, tq), F32)
        for b in range(NUM_BUCKETS):
            val = jnp.where(bucket == b, tab_ref[b, h], val)
        val = (val - far) * LOG2E
        if n == n_near:
            val = jnp.where((col // CHUNK) <= (row // CHUNK), val, -jnp.inf)
        out_ref[0, n - 1] = val


def _bias_tiles(rel_table, tq, tk):
    n_near = _n_near(tk)
    return pl.pallas_call(
        functools.partial(_bias_kernel, tq=tq, tk=tk, n_near=n_near),
        grid=(HEADS,),
        in_specs=[pl.BlockSpec(memory_space=pltpu.SMEM)],
        out_specs=pl.BlockSpec((1, n_near, tk, tq), lambda h: (h, 0, 0, 0)),
        out_shape=jax.ShapeDtypeStruct((HEADS, n_near, tk, tq), F32),
        compiler_params=_params(("arbitrary",)),
        name="bias_tiles",
    )(rel_table)


def _flash_kernel(*refs, tq, tk, past, n_maps, has_bias, has_sel, lam_init, hg, dq):
    refs = list(refs)
    q_ref, k_ref, vt_ref = refs[:3]
    pos = 3
    bias_ref = sel_ref = lam_ref = gsub_ref = None
    if has_bias:
        bias_ref = refs[pos]; pos += 1
    if has_sel:
        sel_ref = refs[pos]; pos += 1
    if n_maps == 2:
        lam_ref, gsub_ref = refs[pos], refs[pos + 1]; pos += 2
    o_ref, m_ref, l_ref, acc_ref = refs[pos:pos + 4]
    n_near = _n_near(tk) if has_bias else 1
    ntq = n_maps * tq

    j_diag = past // tk + pl.program_id(2)
    qs = []
    for h in range(hg):
        q = q_ref[0, :, h * dq:(h + 1) * dq]
        if n_maps == 2:
            lane = lax.broadcasted_iota(jnp.int32, q.shape, 1)
            q = jnp.concatenate([jnp.where(lane < DIFF_DIM, q, jnp.zeros_like(q)),
                                 jnp.where(lane >= DIFF_DIM, q, jnp.zeros_like(q))], axis=0)
        qs.append(q)
    m_ref[...] = jnp.full(m_ref.shape, NEG_BIG, F32)
    l_ref[...] = jnp.zeros(l_ref.shape, F32)
    acc_ref[...] = jnp.zeros(acc_ref.shape, F32)
    def tile(j, near, nt=1):
        start = pl.multiple_of(j * tk, tk)
        ones = jnp.ones((ONES_ROWS, nt * tk), BF)
        extra = None
        if not has_bias and near is not None:
            krow = lax.broadcasted_iota(jnp.int32, (tk, tq), 0)
            qcol = lax.broadcasted_iota(jnp.int32, (tk, tq), 1)
            extra = jnp.where((krow // CHUNK) <= (qcol // CHUNK), 0.0, -jnp.inf)
        if has_sel:
            extra = jnp.concatenate([sel_ref[0, 0, j + t] for t in range(nt)], axis=0).astype(F32)
        ss, pp, aa = {}, {}, {}

        def logits(h):
            k = k_ref[0, pl.ds(start, nt * tk), h * dq:(h + 1) * dq]
            ss[h] = _dot_nt(k, qs[h])

        def softmax(h):
            add = extra
            if has_bias and near is not None:
                add = bias_ref[h, near] if add is None else add + bias_ref[h, near]
            s = ss.pop(h)
            if add is not None:
                s = s + (add if n_maps == 1 else jnp.concatenate([add] * n_maps, axis=1))
            m_old = m_ref[h]
            m_new = jnp.maximum(m_old, jnp.max(s, axis=0, keepdims=True))
            aa[h] = jnp.exp2(m_old - m_new)
            pp[h] = jnp.exp2(s - m_new[:1]).astype(BF)
            m_ref[h] = m_new

        def values(h):
            vt = [vt_ref[j + t, h * HEAD_W:(h + 1) * HEAD_W, :] for t in range(nt)]
            v1t = jnp.concatenate([vt[0] if nt == 1 else jnp.concatenate(vt, axis=1), ones], axis=0)
            pv = _dot(v1t, pp.pop(h))
            a = aa.pop(h)
            l_ref[h] = a * l_ref[h] + pv[HEAD_W:HEAD_W + 8]
            acc_ref[h] = a[:1] * acc_ref[h] + pv[:HEAD_W]

        ahead = min(HEAD_LOOKAHEAD[n_maps], hg)
        for h in range(ahead):
            logits(h)
        behind = 1 if ahead < hg else hg
        for h in range(hg):
            if h + ahead < hg:
                logits(h + ahead)
            softmax(h)
            if h >= behind:
                values(h - behind)
        for h in range(max(hg - behind, 0), hg):
            values(h)

    group = FAR_GROUP[n_maps]
    n_far = jnp.maximum(j_diag - (n_near - 1), 0)

    def far_group(i, carry):
        tile(group * i, None, group)
        return carry

    lax.fori_loop(0, n_far // group, far_group, 0)
    if group == 2:
        pl.when(n_far % 2 == 1)(functools.partial(tile, n_far - 1, None))
    for n in range(n_near):
        j = j_diag - (n_near - 1) + n
        if n == n_near - 1:
            tile(j, n)
        else:
            pl.when(j >= 0)(functools.partial(tile, j, n))

    if n_maps == 2:
        lp = lam_ref[...]
        lam = (jnp.exp(jnp.sum(lp[0:1] * lp[1:2], axis=-1, keepdims=True))
               - jnp.exp(jnp.sum(lp[2:3] * lp[3:4], axis=-1, keepdims=True)) + lam_init)
    for h in range(hg):
        out = acc_ref[h] * (1.0 / l_ref[h][:1])
        if n_maps == 2:
            out = out[:, :tq] - lam * out[:, tq:]
            ms = jnp.mean(out * out, axis=0, keepdims=True)
            out = out * lax.rsqrt(ms + SUBLN_EPS) * gsub_ref[...] * (1.0 - lam_init)
        o_ref[0, :, h * HEAD_W:(h + 1) * HEAD_W] = out.T.astype(o_ref.dtype)


def _flash(q, k, v, *, tq, past, dq, bias=None, sel=None, lam_params=None, g_subln=None,
           lam_init=0.0):
    b, sq, _ = q.shape
    sk = k.shape[1]
    tk = tq
    nk = sk // tk
    n_maps = 2 if lam_params is not None else 1
    hg = HEADS_PER_STEP
    vt = v.reshape(b, nk, tk, HEADS * HEAD_W).transpose(0, 1, 3, 2).reshape(b * nk, HEADS * HEAD_W, tk)
    ins = [q, k, vt]
    once = pl.Buffered(1)
    in_specs = [
        pl.BlockSpec((1, tq, hg * dq), lambda bi, h, i: (bi, i, h)),
        pl.BlockSpec((1, sk, hg * dq), lambda bi, h, i: (bi, 0, h), pipeline_mode=once),
        pl.BlockSpec((nk, hg * HEAD_W, tk), lambda bi, h, i: (bi, h, 0), pipeline_mode=once),
    ]
    if bias is not None:
        ins.append(bias)
        in_specs.append(pl.BlockSpec((hg,) + bias.shape[1:], lambda bi, h, i: (h, 0, 0, 0),
                                     pipeline_mode=once))
    if sel is not None:
        ins.append(sel)
        in_specs.append(pl.BlockSpec((1, 1) + sel.shape[2:], lambda bi, h, i: (bi, i, 0, 0, 0)))
    if n_maps == 2:
        ins += [lam_params, jnp.broadcast_to(g_subln.astype(F32)[:, None], (HEAD_W, tq))]
        in_specs += [pl.BlockSpec(lam_params.shape, lambda bi, h, i: (0, 0)),
                     pl.BlockSpec((HEAD_W, tq), lambda bi, h, i: (0, 0))]
    kern = functools.partial(_flash_kernel, tq=tq, tk=tk, past=past, n_maps=n_maps,
                             has_bias=bias is not None, has_sel=sel is not None,
                             lam_init=lam_init, hg=hg, dq=dq)
    return pl.pallas_call(
        kern,
        grid=(b, HEADS // hg, sq // tq),
        in_specs=in_specs,
        out_specs=pl.BlockSpec((1, tq, hg * HEAD_W), lambda bi, h, i: (bi, i, h)),
        out_shape=jax.ShapeDtypeStruct((b, sq, HEADS * HEAD_W), BF),
        scratch_shapes=[pltpu.VMEM((hg, 8, n_maps * tq), F32), pltpu.VMEM((hg, 8, n_maps * tq), F32),
                        pltpu.VMEM((hg, HEAD_W, n_maps * tq), F32)],
        compiler_params=_params(("parallel", "parallel", "arbitrary")),
        name="flash",
    )(*ins)


def _post_kernel(*refs, n_parts, final):
    x_ref = refs[0]
    part_refs = refs[1:1 + n_parts]
    wout_refs = refs[1 + n_parts:1 + 2 * n_parts]
    gffn_ref, wg_ref, wu_ref, wd_ref = refs[1 + 2 * n_parts:5 + 2 * n_parts]
    gfin_ref = refs[5 + 2 * n_parts] if final else None
    o_ref, act_ref = refs[-2:]
    x = x_ref[...]
    for p in range(n_parts):
        x = x + _dot(part_refs[p][...], wout_refs[p][...])
    h = _rms(x, gffn_ref[...], NORM_EPS).astype(BF)
    for c in range(D_FF // FF_CHUNK):
        sl = slice(c * FF_CHUNK, (c + 1) * FF_CHUNK)
        g = _dot(h, wg_ref[:, sl])
        u = _dot(h, wu_ref[:, sl])
        act_ref[:, sl] = (g * (1.0 / (1.0 + jnp.exp(-g))) * u).astype(BF)
    y = x + _dot(act_ref[...], wd_ref[...])
    if final:
        y = _rms(y, gfin_ref[...], NORM_EPS)
    o_ref[...] = y


def _post(x2d, parts, w_out, g_ffn, w_gate, w_up, w_down, g_final, tm):
    t = x2d.shape[0]
    final = g_final is not None
    row = lambda w: pl.BlockSpec((tm, w), lambda i: (i, 0))
    w_out = w_out.astype(BF)
    consts = [w_out[p * 1024:(p + 1) * 1024] for p in range(len(parts))]
    consts += [g_ffn.reshape(1, -1), w_gate.astype(BF), w_up.astype(BF), w_down.astype(BF)]
    if final:
        consts.append(g_final.reshape(1, -1))
    return pl.pallas_call(
        functools.partial(_post_kernel, n_parts=len(parts), final=final),
        grid=(t // tm,),
        in_specs=[row(D_MODEL)] + [row(1024) for _ in parts] + [_const_spec(c.shape) for c in consts],
        out_specs=row(D_MODEL),
        out_shape=jax.ShapeDtypeStruct((t, D_MODEL), F32),
        scratch_shapes=[pltpu.VMEM((tm, D_FF), BF)],
        compiler_params=_params(("parallel",)),
        name="post",
    )(x2d, *parts, *consts)


def _odd_proj_kernel(x_ref, g_ref, wm_ref, wki_ref, ww_ref,
                     q_ref, k_ref, v_ref, k16_ref, v16_ref, qi_ref, ki_ref, ki16_ref, w_out_ref):
    h = _rms(x_ref[...], g_ref[...], NORM_EPS).astype(BF)
    q_ref[...] = (_dot(h, wm_ref[:, 0:1024]) * DSA_SCALE).astype(BF)
    k = _dot(h, wm_ref[:, 1024:2048])
    k_ref[...] = k
    k16_ref[...] = k.astype(BF)
    v = _dot(h, wm_ref[:, 2048:3072])
    v_ref[...] = v
    v16_ref[...] = v.astype(BF)
    qi_ref[...] = _dot(h, wm_ref[:, 3072:3584]).astype(BF)
    ki = _dot(h, wki_ref[...])
    ki_ref[...] = ki
    ki16_ref[...] = ki.astype(BF)
    w_out_ref[...] = _dot(h, ww_ref[...]) * IDX_SCALE


def _odd_proj(x2d, tm, g_attn, w_in):
    t = x2d.shape[0]
    c0 = 3 * 1024 + HEADS * IDX_DIM
    wm = w_in[:, :c0].astype(BF)
    wki = w_in[:, c0:c0 + IDX_DIM].astype(BF)
    ww = jnp.pad(w_in[:, c0 + IDX_DIM:], ((0, 0), (0, LANES - HEADS))).astype(BF)
    row = lambda w: pl.BlockSpec((tm, w), lambda i: (i, 0))
    out_shapes = [
        jax.ShapeDtypeStruct((t, 1024), BF),
        jax.ShapeDtypeStruct((t, 1024), F32),
        jax.ShapeDtypeStruct((t, 1024), F32),
        jax.ShapeDtypeStruct((t, 1024), BF),
        jax.ShapeDtypeStruct((t, 1024), BF),
        jax.ShapeDtypeStruct((t, HEADS * IDX_DIM), BF),
        jax.ShapeDtypeStruct((t, IDX_DIM), F32),
        jax.ShapeDtypeStruct((t, IDX_DIM), BF),
        jax.ShapeDtypeStruct((t, LANES), F32),
    ]
    return pl.pallas_call(
        _odd_proj_kernel,
        grid=(t // tm,),
        in_specs=[row(D_MODEL), _const_spec((1, D_MODEL)), _const_spec(wm.shape),
                  _const_spec(wki.shape), _const_spec(ww.shape)],
        out_specs=[row(s.shape[1]) for s in out_shapes],
        out_shape=out_shapes,
        compiler_params=_params(("parallel",)),
        name="odd_proj",
    )(x2d, g_attn.reshape(1, -1), wm, wki, ww)


def _indexer_kernel(qi_ref, wt_ref, k_ref, sel_ref, key_ref, hi_ref, cut_ref, *, tq, tk, past, top_k,
                    nk, idx_bits):
    qblk = pl.program_id(1)
    nkv = past // tk + qblk + 1
    q = qi_ref[0]
    wt = wt_ref[0]
    qh = [q[:, h * IDX_DIM:(h + 1) * IDX_DIM] for h in range(HEADS)]
    krow = lax.broadcasted_iota(jnp.int32, (tk, tq), 0)
    qcol = lax.broadcasted_iota(jnp.int32, (tk, tq), 1)
    kf = float(top_k)

    def keys_of(j, diagonal):
        k = k_ref[0, pl.ds(pl.multiple_of(j * tk, tk), tk), :]
        dots = [_dot_nt(k, qh[h]) for h in range(HEADS)]
        sc = jnp.zeros((tk, tq), F32)
        for h in range(HEADS):
            sc = sc + wt[h:h + 1, :] * jnp.maximum(dots[h], 0.0)
        sc = jnp.where(sc == 0.0, 0.0, sc)
        bits = pltpu.bitcast(sc, jnp.int32)
        key = jnp.where(bits < 0, bits ^ 0x7FFFFFFF, bits)
        if diagonal:
            key = jnp.where((krow // CHUNK) <= (qcol // CHUNK), key, INT_MIN)
        return key

    def put_keys(j, diagonal):
        key = keys_of(j, diagonal)
        key_ref[j] = key
        hi_ref[j] = lax.shift_right_arithmetic(key, 16).astype(jnp.int16)

    def score_pair(i, carry):
        put_keys(2 * i, False)
        put_keys(2 * i + 1, False)
        return carry

    n_far = nkv - 1
    lax.fori_loop(0, n_far // 2, score_pair, 0)
    pl.when(n_far % 2 == 1)(functools.partial(put_keys, n_far - 1, False))
    put_keys(nkv - 1, True)

    def count(pred):
        def body(j, acc):
            hit = jnp.where(pred(key_ref[j], j * tk + krow), 1.0, 0.0)
            for r in range(tk // ACC_ROWS):
                acc = acc + hit[r * ACC_ROWS:(r + 1) * ACC_ROWS]
            return acc
        acc = lax.fori_loop(0, nkv, body, jnp.zeros((ACC_ROWS, tq), F32))
        return jnp.sum(acc, axis=0, keepdims=True)

    def count_hi(cand_hi):
        cand16 = cand_hi.astype(jnp.int16)
        def body(j, acc):
            hit = jnp.where(hi_ref[j] >= cand16, jnp.int16(1), jnp.int16(0))
            for r in range(tk // ACC_ROWS):
                acc = acc + hit[r * ACC_ROWS:(r + 1) * ACC_ROWS]
            return acc
        acc = lax.fori_loop(0, nkv, body, jnp.zeros((ACC_ROWS, tq), jnp.int16))
        return jnp.sum(acc.astype(F32), axis=0, keepdims=True)

    c0 = count_hi(jnp.zeros((1, tq), jnp.int32))
    thr_hi = jnp.where(c0 >= kf, 0, -2 ** 15).astype(jnp.int32)

    def hi_body(i, thr_hi):
        cand = thr_hi + lax.shift_left(jnp.int32(1), 14 - i)
        return jnp.where(count_hi(cand) >= kf, cand, thr_hi)

    thr_hi = lax.fori_loop(0, 15, hi_body, thr_hi)
    thr = lax.shift_left(thr_hi, 16)

    def bit_body(i, thr):
        cand = thr + lax.shift_left(jnp.int32(1), 15 - i)
        c = count(lambda key, idx: key >= cand)
        return jnp.where(c >= kf, cand, thr)

    thr = lax.fori_loop(0, 16, bit_body, thr)

    c_gt = count(lambda key, idx: key > thr)
    c_ge = count(lambda key, idx: key >= thr)
    need = kf - c_gt
    real = thr > KEY_NEG_INF
    excess = real & (c_ge - c_gt > need)
    any_excess = jnp.max(jnp.where(excess, 1.0, 0.0)) > 0.0

    def emit(keep_fn):
        def body(j, carry):
            keep = keep_fn(key_ref[j], j * tk + krow)
            sel_ref[0, 0, j] = jnp.where(keep, 0.0, -jnp.inf).astype(BF)
            return carry
        lax.fori_loop(0, nkv, body, 0)

    @pl.when(jnp.logical_not(any_excess))
    def _():
        thr1 = jnp.maximum(thr, KEY_NEG_INF + 1)
        emit(lambda key, idx: key >= thr1)

    @pl.when(any_excess)
    def _():
        def idx_body(i, cut):
            cand = cut + lax.shift_left(jnp.int32(1), idx_bits - 1 - i)
            c = count(lambda key, idx: (key == thr) & (idx < cand))
            return jnp.where(c < need, cand, cut)
        cut_ref[...] = lax.fori_loop(0, idx_bits, idx_body, jnp.zeros((1, tq), jnp.int32))
        cut = cut_ref[...]
        floor = jnp.maximum(thr, KEY_NEG_INF)
        emit(lambda key, idx: (key > floor) | ((key == thr) & real & (idx <= cut)))

    def zero_body(j, carry):
        sel_ref[0, 0, j] = jnp.zeros((tk, tq), BF)
        return carry

    lax.fori_loop(nkv, nk, zero_body, 0)


def _indexer(qidx, widx, kidx16, *, tq, past, top_k):
    b, sq, _ = qidx.shape
    sk = kidx16.shape[1]
    tk = tq
    nk = sk // tk
    idx_bits = max(1, (sk - 1).bit_length())
    kern = functools.partial(_indexer_kernel, tq=tq, tk=tk, past=past, top_k=top_k, nk=nk,
                             idx_bits=idx_bits)
    widx_t = widx[:, :, :HEADS].transpose(0, 2, 1)
    return pl.pallas_call(
        kern,
        grid=(b, sq // tq),
        in_specs=[pl.BlockSpec((1, tq, HEADS * IDX_DIM), lambda bi, i: (bi, i, 0)),
                  pl.BlockSpec((1, HEADS, tq), lambda bi, i: (bi, 0, i)),
                  pl.BlockSpec((1, sk, IDX_DIM), lambda bi, i: (bi, 0, 0))],
        out_specs=pl.BlockSpec((1, 1, nk, tk, tq), lambda bi, i: (bi, i, 0, 0, 0)),
        out_shape=jax.ShapeDtypeStruct((b, sq // tq, nk, tk, tq), BF),
        scratch_shapes=[pltpu.VMEM((nk, tk, tq), jnp.int32), pltpu.VMEM((nk, tk, tq), jnp.int16),
                        pltpu.VMEM((1, tq), jnp.int32)],
        compiler_params=_params(("parallel", "arbitrary")),
        name="indexer",
    )(qidx, widx_t, kidx16)


def _rope_tables(pos):
    half = MLA_ROPE // 2
    inv = ROPE_THETA ** (-jnp.arange(half, dtype=F32) / half)
    ang = pos.astype(F32)[:, None] * inv[None, :]
    pad = jnp.zeros((pos.shape[0], HEAD_W - MLA_ROPE), F32)
    cos, sin = jnp.cos(ang), jnp.sin(ang)
    return (jnp.concatenate([cos, cos, pad], axis=1), jnp.concatenate([sin, sin, pad], axis=1))


def _row_tile(seq):
    return ROW_TILE if seq % ROW_TILE == 0 else CHUNK


def _pad_seq(a):
    n = -a.shape[1] % ATT_TILE
    return a if n == 0 else jnp.pad(a, ((0, 0), (0, n), (0, 0)))


def _even_layer(x, past_len, past, bias, p, lam_init):
    b, s, _ = x.shape
    tm = _row_tile(s)
    pos = past_len + jnp.arange(s, dtype=jnp.int32)
    cos_p, sin_p = _rope_tables(pos)
    x2d = x.reshape(b * s, D_MODEL)
    qcat, ckv, kpe, dq, dk, dv, dk16, dv16 = _even_proj(
        x2d, s, tm, p["g_attn"], p["w_in"], p["g_q"], p["w_q_up"], p["g_kv"], cos_p, sin_p)
    r3 = lambda a: a.reshape(b, s, -1)
    if past is None:
        ckv_all, kpe_all, dk_all, dv_all = r3(ckv), r3(kpe), r3(dk16), r3(dv16)
    else:
        c_ckv, c_kpe, c_dk, c_dv = past
        ckv_all = jnp.concatenate([c_ckv, r3(ckv)], axis=1)
        kpe_all = jnp.concatenate([c_kpe, r3(kpe)], axis=1)
        dk_all = jnp.concatenate([c_dk.reshape(b, past_len, -1).astype(BF), r3(dk16)], axis=1)
        dv_all = jnp.concatenate([c_dv.reshape(b, past_len, -1).astype(BF), r3(dv16)], axis=1)
    ckv_all, kpe_all, dk_all, dv_all = map(_pad_seq, (ckv_all, kpe_all, dk_all, dv_all))
    sk = ckv_all.shape[1]
    kcat, vmla = _kv_up(ckv_all.reshape(b * sk, KV_LORA), kpe_all.reshape(b * sk, MLA_ROPE),
                        p["w_kv_up"], ROW_TILE if (b * sk) % ROW_TILE == 0 else ATT_TILE)
    a_out = _flash(_pad_seq(r3(qcat)), kcat.reshape(b, sk, -1), vmla.reshape(b, sk, -1),
                   tq=ATT_TILE, past=past_len, dq=MLA_QK)
    lam_params = jnp.stack([p["lq1"], p["lk1"], p["lq2"], p["lk2"]]).astype(F32)
    b_out = _flash(_pad_seq(r3(dq)), dk_all, dv_all, tq=ATT_TILE, past=past_len, dq=HEAD_W, bias=bias,
                   lam_params=lam_params, g_subln=p["g_subln"], lam_init=lam_init)
    parts = [a_out[:, :s].reshape(b * s, -1), b_out[:, :s].reshape(b * s, -1)]
    new = (r3(ckv), r3(kpe), dk.reshape(b, s, HEADS, 2 * DIFF_DIM), dv.reshape(b, s, HEADS, 2 * DIFF_DIM))
    return x2d, parts, new


def _odd_layer(x, past_len, past, bias, p):
    b, s, _ = x.shape
    tm = _row_tile(s)
    x2d = x.reshape(b * s, D_MODEL)
    q, k, v, k16, v16, qi, ki, ki16, widx = _odd_proj(x2d, tm, p["g_attn"], p["w_in"])
    r3 = lambda a: a.reshape(b, s, -1)
    if past is None:
        k_all, v_all, ki_all = r3(k16), r3(v16), r3(ki16)
    else:
        c_k, c_v, c_ki = past
        k_all = jnp.concatenate([c_k.reshape(b, past_len, -1).astype(BF), r3(k16)], axis=1)
        v_all = jnp.concatenate([c_v.reshape(b, past_len, -1).astype(BF), r3(v16)], axis=1)
        ki_all = jnp.concatenate([c_ki.astype(BF), r3(ki16)], axis=1)
    top_k = min(DSA_TOPK, k_all.shape[1] // 4)
    k_all, v_all, ki_all = map(_pad_seq, (k_all, v_all, ki_all))
    sel = _indexer(_pad_seq(r3(qi)), _pad_seq(r3(widx)), ki_all, tq=ATT_TILE, past=past_len, top_k=top_k)
    out = _flash(_pad_seq(r3(q)), k_all, v_all, tq=ATT_TILE, past=past_len, dq=HEAD_W, bias=bias, sel=sel)
    new = (k.reshape(b, s, HEADS, DSA_DIM), v.reshape(b, s, HEADS, DSA_DIM), r3(ki))
    return x2d, [out[:, :s].reshape(b * s, -1)], new


def kernel(x_prompt, x_sample, cache_mla_ckv, cache_mla_kpe, cache_diff_k, cache_diff_v, cache_dsa_k, cache_dsa_v, cache_dsa_kidx, g_attn_even, w_in_even, g_q_lora, w_q_up, g_kv_lora, w_kv_up, lambda_q1, lambda_k1, lambda_q2, lambda_k2, g_diff_subln, w_out_even, g_attn_odd, w_in_odd, w_out_odd, rel_bias_table, g_ffn, w_gate, w_up, w_down, g_final):
    depth = g_ffn.shape[0]
    past_len = cache_mla_ckv.shape[2]
    groups = [
        dict(x=x_prompt, past_len=0, has_past=False),
        dict(x=x_sample, past_len=past_len, has_past=True),
    ]
    bias = _bias_tiles(rel_bias_table, ATT_TILE, ATT_TILE)
    results = []
    for grp in groups:
        x = grp["x"]
        b, s, _ = x.shape
        tm = _row_tile(s)
        even_new, odd_new = [], []
        for layer in range(depth):
            if layer % 2 == 0:
                e = layer // 2
                lam_init = 0.8 - 0.6 * math.exp(-0.3 * layer)
                p = dict(g_attn=g_attn_even[e], w_in=w_in_even[e], g_q=g_q_lora[e], w_q_up=w_q_up[e],
                         g_kv=g_kv_lora[e], w_kv_up=w_kv_up[e], lq1=lambda_q1[e], lk1=lambda_k1[e],
                         lq2=lambda_q2[e], lk2=lambda_k2[e], g_subln=g_diff_subln[e])
                past = ((cache_mla_ckv[e], cache_mla_kpe[e], cache_diff_k[e], cache_diff_v[e])
                        if grp["has_past"] else None)
                x2d, parts, new = _even_layer(x, grp["past_len"], past, bias, p, lam_init)
                even_new.append(new)
                w_out = w_out_even[e]
            else:
                o = layer // 2
                p = dict(g_attn=g_attn_odd[o], w_in=w_in_odd[o])
                past = ((cache_dsa_k[o], cache_dsa_v[o], cache_dsa_kidx[o])
                        if grp["has_past"] else None)
                x2d, parts, new = _odd_layer(x, grp["past_len"], past, bias, p)
                odd_new.append(new)
                w_out = w_out_odd[o]
            gfin = g_final if layer == depth - 1 else None
            x = _post(x2d, parts, w_out, g_ffn[layer], w_gate[layer], w_up[layer], w_down[layer],
                      gfin, tm).reshape(b, s, D_MODEL)
        results.append((x, [jnp.stack(a) for a in zip(*even_new)], [jnp.stack(a) for a in zip(*odd_new)]))
    (yp, ep, op), (ys, es, os_) = results
    return (yp, ys, *ep, *op, *es, *os_)
```

```python
import functools
import math

import jax
import jax.numpy as jnp
from jax import lax
from jax.experimental import pallas as pl
from jax.experimental.pallas import tpu as pltpu

D_MODEL = 1024
CHUNK = 64
NORM_EPS = 1e-6
SUBLN_EPS = 1e-5
HEADS = 8
MLA_NOPE = 128
MLA_ROPE = 64
MLA_V = 128
Q_LORA = 384
KV_LORA = 256
ROPE_THETA = 10000.0
LOG2E = math.log2(math.e)
MLA_SCALE = (MLA_NOPE + MLA_ROPE) ** -0.5 * LOG2E
DIFF_DIM = 64
DIFF_SCALE = DIFF_DIM ** -0.5 * LOG2E
DSA_DIM = 128
DSA_SCALE = DSA_DIM ** -0.5 * LOG2E
IDX_DIM = 64
IDX_SCALE = IDX_DIM ** -0.5 * HEADS ** -0.5
DSA_TOPK = 256
NUM_BUCKETS = 32
T5_FAR = 91
HEAD_W = 128
MLA_QK = 256
D_FF = -(-8 * D_MODEL // (3 * 256)) * 256
FF_CHUNK = 256
HEADS_PER_STEP = 8
ATT_TILE = 256
HEAD_LOOKAHEAD = {1: 8, 2: 8}
FAR_GROUP = {1: 2, 2: 1}
ROW_TILE = 512
ONES_ROWS = 16

LANES = 128
VMEM_LIMIT = 56 * 1024 * 1024

INT_MIN = -2 ** 31
KEY_NEG_INF = (0xFF800000 ^ 0x7FFFFFFF) - 2 ** 32
ACC_ROWS = 32
NEG_BIG = -1e30
BF = jnp.bfloat16
F32 = jnp.float32


def _dot(a, b):
    return jnp.dot(a, b, preferred_element_type=F32)


def _dot_nt(a, b):
    return lax.dot_general(a, b, (((1,), (1,)), ((), ())), preferred_element_type=F32)


def _rms(x, g, eps):
    return x * lax.rsqrt(jnp.mean(x * x, axis=-1, keepdims=True) + eps) * g


def _params(sem):
    return pltpu.CompilerParams(dimension_semantics=sem, vmem_limit_bytes=VMEM_LIMIT)


def _const_spec(shape):
    nd = len(shape)
    return pl.BlockSpec(shape, lambda *_: (0,) * nd, pipeline_mode=pl.Buffered(1))


def _even_proj_kernel(x_ref, g_ref, wq_ref, wkv_ref, wd_ref, wkpe_ref, wkrot_ref, gq_ref,
                      wqcat_ref, wqrot_ref, gkv_ref, cos_ref, sin_ref,
                      qcat_ref, ckv_ref, kpe_ref, dq_ref, dk_ref, dv_ref, dk16_ref, dv16_ref):
    h = _rms(x_ref[...], g_ref[...], NORM_EPS).astype(BF)
    cos = cos_ref[...]
    sin = sin_ref[...]
    qn = _rms(_dot(h, wq_ref[...]), gq_ref[...], NORM_EPS).astype(BF)
    qc = _dot(qn, wqcat_ref[...])
    qr = _dot(qn, wqrot_ref[...])
    for hh in range(HEADS):
        a = hh * MLA_QK
        qcat_ref[:, a:a + HEAD_W] = (qc[:, a:a + HEAD_W] * MLA_SCALE).astype(BF)
        pe = qc[:, a + HEAD_W:a + MLA_QK] * cos + qr[:, hh * HEAD_W:(hh + 1) * HEAD_W] * sin
        qcat_ref[:, a + HEAD_W:a + MLA_QK] = (pe * MLA_SCALE).astype(BF)
    ckv_ref[...] = _rms(_dot(h, wkv_ref[...]), gkv_ref[...], NORM_EPS)
    kpe_ref[...] = (_dot(h, wkpe_ref[...]) * cos[:, :MLA_ROPE]
                    + _dot(h, wkrot_ref[...]) * sin[:, :MLA_ROPE])
    dq_ref[...] = (_dot(h, wd_ref[:, 0:1024]) * DIFF_SCALE).astype(BF)
    dk = _dot(h, wd_ref[:, 1024:2048])
    dk_ref[...] = dk
    dk16_ref[...] = dk.astype(BF)
    dv = _dot(h, wd_ref[:, 2048:3072])
    dv_ref[...] = dv
    dv16_ref[...] = dv.astype(BF)


def _even_proj(x2d, seq, tm, g_attn, w_in, g_q, w_q_up, g_kv, cos_p, sin_p):
    t = x2d.shape[0]
    c0, c1, c2 = Q_LORA, Q_LORA + KV_LORA, Q_LORA + KV_LORA + MLA_ROPE
    half = MLA_ROPE // 2
    wq = w_in[:, :c0].astype(BF)
    wkv = w_in[:, c0:c1].astype(BF)
    wkpe = w_in[:, c1:c2]
    wkrot = jnp.concatenate([-wkpe[:, half:], wkpe[:, :half]], axis=1).astype(BF)
    wkpe = wkpe.astype(BF)
    wd = w_in[:, c2:].astype(BF)
    wqu = w_q_up.reshape(Q_LORA, HEADS, MLA_NOPE + MLA_ROPE)
    w_nope, w_pe = wqu[..., :MLA_NOPE], wqu[..., MLA_NOPE:]
    w_rot = jnp.concatenate([-w_pe[..., half:], w_pe[..., :half]], axis=-1)
    zpad = jnp.zeros((Q_LORA, HEADS, MLA_QK - MLA_NOPE - MLA_ROPE), F32)
    wqcat = jnp.concatenate([w_nope, w_pe, zpad], axis=-1).reshape(Q_LORA, HEADS * MLA_QK).astype(BF)
    wqrot = jnp.concatenate([w_rot, zpad], axis=-1).reshape(Q_LORA, HEADS * HEAD_W).astype(BF)
    n_pos = seq // tm
    row = lambda w: pl.BlockSpec((tm, w), lambda i: (i, 0))
    pos = pl.BlockSpec((tm, HEAD_W), lambda i: (i % n_pos, 0))
    out_shapes = [
        jax.ShapeDtypeStruct((t, HEADS * MLA_QK), BF),
        jax.ShapeDtypeStruct((t, KV_LORA), F32),
        jax.ShapeDtypeStruct((t, MLA_ROPE), F32),
        jax.ShapeDtypeStruct((t, 1024), BF),
        jax.ShapeDtypeStruct((t, 1024), F32),
        jax.ShapeDtypeStruct((t, 1024), F32),
        jax.ShapeDtypeStruct((t, 1024), BF),
        jax.ShapeDtypeStruct((t, 1024), BF),
    ]
    ins = [x2d, g_attn.reshape(1, -1), wq, wkv, wd, wkpe, wkrot, g_q.reshape(1, -1), wqcat, wqrot,
           g_kv.reshape(1, -1), cos_p, sin_p]
    in_specs = [row(D_MODEL)] + [_const_spec(a.shape) for a in ins[1:11]] + [pos, pos]
    return pl.pallas_call(
        _even_proj_kernel,
        grid=(t // tm,),
        in_specs=in_specs,
        out_specs=[row(s.shape[1]) for s in out_shapes],
        out_shape=out_shapes,
        compiler_params=_params(("parallel",)),
        name="even_proj",
    )(*ins)


def _kv_up_kernel(ckv_ref, kpe_ref, wk_ref, wv_ref, kcat_ref, v_ref):
    c = ckv_ref[...].astype(BF)
    kn = _dot(c, wk_ref[...])
    v_ref[...] = _dot(c, wv_ref[...]).astype(BF)
    kpe = kpe_ref[...]
    kp = jnp.concatenate([kpe, jnp.zeros_like(kpe)], axis=-1).astype(BF)
    for hh in range(HEADS):
        a = hh * MLA_QK
        kcat_ref[:, a:a + HEAD_W] = kn[:, hh * HEAD_W:(hh + 1) * HEAD_W].astype(BF)
        kcat_ref[:, a + HEAD_W:a + MLA_QK] = kp


def _kv_up(ckv2d, kpe2d, w_kv_up, tm):
    t = ckv2d.shape[0]
    wkv = w_kv_up.reshape(KV_LORA, HEADS, MLA_NOPE + MLA_V)
    wk = wkv[..., :MLA_NOPE].reshape(KV_LORA, HEADS * MLA_NOPE).astype(BF)
    wv = wkv[..., MLA_NOPE:].reshape(KV_LORA, HEADS * MLA_V).astype(BF)
    row = lambda w: pl.BlockSpec((tm, w), lambda i: (i, 0))
    return pl.pallas_call(
        _kv_up_kernel,
        grid=(t // tm,),
        in_specs=[row(KV_LORA), row(MLA_ROPE), _const_spec(wk.shape), _const_spec(wv.shape)],
        out_specs=[row(HEADS * MLA_QK), row(HEADS * MLA_V)],
        out_shape=[jax.ShapeDtypeStruct((t, HEADS * MLA_QK), BF),
                   jax.ShapeDtypeStruct((t, HEADS * MLA_V), BF)],
        compiler_params=_params(("parallel",)),
        name="kv_up",
    )(ckv2d, kpe2d, wk, wv)


def _n_near(tk):
    return 1 + -(-(T5_FAR - 1) // tk)


def _bias_kernel(tab_ref, out_ref, *, tq, tk, n_near):
    h = pl.program_id(0)
    col = lax.broadcasted_iota(jnp.int32, (tk, tq), 0)
    row = lax.broadcasted_iota(jnp.int32, (tk, tq), 1)
    far = tab_ref[NUM_BUCKETS // 2 - 1, h]
    for n in range(1, n_near + 1):
        rel = col - row - (n_near - n) * tk
        dist = jnp.abs(rel)
        large = jnp.full((tk, tq), NUM_BUCKETS // 4, jnp.int32)
        for thr in (12, 16, 23, 32, 46, 64, T5_FAR):
            large = large + jnp.where(dist >= thr, 1, 0)
        bucket = jnp.where(rel > 0, NUM_BUCKETS // 2, 0) + jnp.where(dist < NUM_BUCKETS // 4, dist, large)
        val = jnp.zeros((tk, tq), F32)
        for b in range(NUM_BUCKETS):
            val = jnp.where(bucket == b, tab_ref[b, h], val)
        val = (val - far) * LOG2E
        if n == n_near:
            val = jnp.where((col // CHUNK) <= (row // CHUNK), val, -jnp.inf)
        out_ref[0, n - 1] = val


def _bias_tiles(rel_table, tq, tk):
    n_near = _n_near(tk)
    return pl.pallas_call(
        functools.partial(_bias_kernel, tq=tq, tk=tk, n_near=n_near),
        grid=(HEADS,),
        in_specs=[pl.BlockSpec(memory_space=pltpu.SMEM)],
        out_specs=pl.BlockSpec((1, n_near, tk, tq), lambda h: (h, 0, 0, 0)),
        out_shape=jax.ShapeDtypeStruct((HEADS, n_near, tk, tq), F32),
        compiler_params=_params(("arbitrary",)),
        name="bias_tiles",
    )(rel_table)


def _flash_kernel(*refs, tq, tk, past, n_maps, has_bias, has_sel, lam_init, hg, dq):
    refs = list(refs)
    q_ref, k_ref, vt_ref = refs[:3]
    pos = 3
    bias_ref = sel_ref = lam_ref = gsub_ref = None
    if has_bias:
        bias_ref = refs[pos]; pos += 1
    if has_sel:
        sel_ref = refs[pos]; pos += 1
    if n_maps == 2:
        lam_ref, gsub_ref = refs[pos], refs[pos + 1]; pos += 2
    o_ref, m_ref, l_ref, acc_ref = refs[pos:pos + 4]
    n_near = _n_near(tk) if has_bias else 1
    ntq = n_maps * tq

    j_diag = past // tk + pl.program_id(2)
    qs = []
    for h in range(hg):
        q = q_ref[0, :, h * dq:(h + 1) * dq]
        if n_maps == 2:
            lane = lax.broadcasted_iota(jnp.int32, q.shape, 1)
            q = jnp.concatenate([jnp.where(lane < DIFF_DIM, q, jnp.zeros_like(q)),
                                 jnp.where(lane >= DIFF_DIM, q, jnp.zeros_like(q))], axis=0)
        qs.append(q)
    m_ref[...] = jnp.full(m_ref.shape, NEG_BIG, F32)
    l_ref[...] = jnp.zeros(l_ref.shape, F32)
    acc_ref[...] = jnp.zeros(acc_ref.shape, F32)
    def tile(j, near, nt=1):
        start = pl.multiple_of(j * tk, tk)
        ones = jnp.ones((ONES_ROWS, nt * tk), BF)
        extra = None
        if not has_bias and near is not None:
            krow = lax.broadcasted_iota(jnp.int32, (tk, tq), 0)
            qcol = lax.broadcasted_iota(jnp.int32, (tk, tq), 1)
            extra = jnp.where((krow // CHUNK) <= (qcol // CHUNK), 0.0, -jnp.inf)
        if has_sel:
            extra = jnp.concatenate([sel_ref[0, 0, j + t] for t in range(nt)], axis=0).astype(F32)
        ss, pp, aa = {}, {}, {}

        def logits(h):
            k = k_ref[0, pl.ds(start, nt * tk), h * dq:(h + 1) * dq]
            ss[h] = _dot_nt(k, qs[h])

        def softmax(h):
            add = extra
            if has_bias and near is not None:
                add = bias_ref[h, near] if add is None else add + bias_ref[h, near]
            s = ss.pop(h)
            if add is not None:
                s = s + (add if n_maps == 1 else jnp.concatenate([add] * n_maps, axis=1))
            m_old = m_ref[h]
            m_new = jnp.maximum(m_old, jnp.max(s, axis=0, keepdims=True))
            aa[h] = jnp.exp2(m_old - m_new)
            pp[h] = jnp.exp2(s - m_new[:1]).astype(BF)
            m_ref[h] = m_new

        def values(h):
            vt = [vt_ref[j + t, h * HEAD_W:(h + 1) * HEAD_W, :] for t in range(nt)]
            v1t = jnp.concatenate([vt[0] if nt == 1 else jnp.concatenate(vt, axis=1), ones], axis=0)
            pv = _dot(v1t, pp.pop(h))
            a = aa.pop(h)
            l_ref[h] = a * l_ref[h] + pv[HEAD_W:HEAD_W + 8]
            acc_ref[h] = a[:1] * acc_ref[h] + pv[:HEAD_W]

        ahead = min(HEAD_LOOKAHEAD[n_maps], hg)
        for h in range(ahead):
            logits(h)
        behind = 1 if ahead < hg else hg
        for h in range(hg):
            if h + ahead < hg:
                logits(h + ahead)
            softmax(h)
            if h >= behind:
                values(h - behind)
        for h in range(max(hg - behind, 0), hg):
            values(h)

    group = FAR_GROUP[n_maps]
    n_far = jnp.maximum(j_diag - (n_near - 1), 0)

    def far_group(i, carry):
        tile(group * i, None, group)
        return carry

    lax.fori_loop(0, n_far // group, far_group, 0)
    if group == 2:
        pl.when(n_far % 2 == 1)(functools.partial(tile, n_far - 1, None))
    for n in range(n_near):
        j = j_diag - (n_near - 1) + n
        if n == n_near - 1:
            tile(j, n)
        else:
            pl.when(j >= 0)(functools.partial(tile, j, n))

    if n_maps == 2:
        lp = lam_ref[...]
        lam = (jnp.exp(jnp.sum(lp[0:1] * lp[1:2], axis=-1, keepdims=True))
               - jnp.exp(jnp.sum(lp[2:3] * lp[3:4], axis=-1, keepdims=True)) + lam_init)
    for h in range(hg):
        out = acc_ref[h] * (1.0 / l_ref[h][:1])
        if n_maps == 2:
            out = out[:, :tq] - lam * out[:, tq:]
            ms = jnp.mean(out * out, axis=0, keepdims=True)
            out = out * lax.rsqrt(ms + SUBLN_EPS) * gsub_ref[...] * (1.0 - lam_init)
        o_ref[0, :, h * HEAD_W:(h + 1) * HEAD_W] = out.T.astype(o_ref.dtype)


def _flash(q, k, v, *, tq, past, dq, bias=None, sel=None, lam_params=None, g_subln=None,
           lam_init=0.0):
    b, sq, _ = q.shape
    sk = k.shape[1]
    tk = tq
    nk = sk // tk
    n_maps = 2 if lam_params is not None else 1
    hg = HEADS_PER_STEP
    vt = v.reshape(b, nk, tk, HEADS * HEAD_W).transpose(0, 1, 3, 2).reshape(b * nk, HEADS * HEAD_W, tk)
    ins = [q, k, vt]
    once = pl.Buffered(1)
    in_specs = [
        pl.BlockSpec((1, tq, hg * dq), lambda bi, h, i: (bi, i, h)),
        pl.BlockSpec((1, sk, hg * dq), lambda bi, h, i: (bi, 0, h), pipeline_mode=once),
        pl.BlockSpec((nk, hg * HEAD_W, tk), lambda bi, h, i: (bi, h, 0), pipeline_mode=once),
    ]
    if bias is not None:
        ins.append(bias)
        in_specs.append(pl.BlockSpec((hg,) + bias.shape[1:], lambda bi, h, i: (h, 0, 0, 0),
                                     pipeline_mode=once))
    if sel is not None:
        ins.append(sel)
        in_specs.append(pl.BlockSpec((1, 1) + sel.shape[2:], lambda bi, h, i: (bi, i, 0, 0, 0)))
    if n_maps == 2:
        ins += [lam_params, jnp.broadcast_to(g_subln.astype(F32)[:, None], (HEAD_W, tq))]
        in_specs += [pl.BlockSpec(lam_params.shape, lambda bi, h, i: (0, 0)),
                     pl.BlockSpec((HEAD_W, tq), lambda bi, h, i: (0, 0))]
    kern = functools.partial(_flash_kernel, tq=tq, tk=tk, past=past, n_maps=n_maps,
                             has_bias=bias is not None, has_sel=sel is not None,
                             lam_init=lam_init, hg=hg, dq=dq)
    return pl.pallas_call(
        kern,
        grid=(b, HEADS // hg, sq // tq),
        in_specs=in_specs,
        out_specs=pl.BlockSpec((1, tq, hg * HEAD_W), lambda bi, h, i: (bi, i, h)),
        out_shape=jax.ShapeDtypeStruct((b, sq, HEADS * HEAD_W), BF),
        scratch_shapes=[pltpu.VMEM((hg, 8, n_maps * tq), F32), pltpu.VMEM((hg, 8, n_maps * tq), F32),
                        pltpu.VMEM((hg, HEAD_W, n_maps * tq), F32)],
        compiler_params=_params(("parallel", "parallel", "arbitrary")),
        name="flash",
    )(*ins)


def _post_kernel(*refs, n_parts, final):
    x_ref = refs[0]
    part_refs = refs[1:1 + n_parts]
    wout_refs = refs[1 + n_parts:1 + 2 * n_parts]
    gffn_ref, wg_ref, wu_ref, wd_ref = refs[1 + 2 * n_parts:5 + 2 * n_parts]
    gfin_ref = refs[5 + 2 * n_parts] if final else None
    o_ref, act_ref = refs[-2:]
    x = x_ref[...]
    for p in range(n_parts):
        x = x + _dot(part_refs[p][...], wout_refs[p][...])
    h = _rms(x, gffn_ref[...], NORM_EPS).astype(BF)
    for c in range(D_FF // FF_CHUNK):
        sl = slice(c * FF_CHUNK, (c + 1) * FF_CHUNK)
        g = _dot(h, wg_ref[:, sl])
        u = _dot(h, wu_ref[:, sl])
        act_ref[:, sl] = (g * (1.0 / (1.0 + jnp.exp(-g))) * u).astype(BF)
    y = x + _dot(act_ref[...], wd_ref[...])
    if final:
        y = _rms(y, gfin_ref[...], NORM_EPS)
    o_ref[...] = y


def _post(x2d, parts, w_out, g_ffn, w_gate, w_up, w_down, g_final, tm):
    t = x2d.shape[0]
    final = g_final is not None
    row = lambda w: pl.BlockSpec((tm, w), lambda i: (i, 0))
    w_out = w_out.astype(BF)
    consts = [w_out[p * 1024:(p + 1) * 1024] for p in range(len(parts))]
    consts += [g_ffn.reshape(1, -1), w_gate.astype(BF), w_up.astype(BF), w_down.astype(BF)]
    if final:
        consts.append(g_final.reshape(1, -1))
    return pl.pallas_call(
        functools.partial(_post_kernel, n_parts=len(parts), final=final),
        grid=(t // tm,),
        in_specs=[row(D_MODEL)] + [row(1024) for _ in parts] + [_const_spec(c.shape) for c in consts],
        out_specs=row(D_MODEL),
        out_shape=jax.ShapeDtypeStruct((t, D_MODEL), F32),
        scratch_shapes=[pltpu.VMEM((tm, D_FF), BF)],
        compiler_params=_params(("parallel",)),
        name="post",
    )(x2d, *parts, *consts)


def _odd_proj_kernel(x_ref, g_ref, wm_ref, wki_ref, ww_ref,
                     q_ref, k_ref, v_ref, k16_ref, v16_ref, qi_ref, ki_ref, ki16_ref, w_out_ref):
    h = _rms(x_ref[...], g_ref[...], NORM_EPS).astype(BF)
    q_ref[...] = (_dot(h, wm_ref[:, 0:1024]) * DSA_SCALE).astype(BF)
    k = _dot(h, wm_ref[:, 1024:2048])
    k_ref[...] = k
    k16_ref[...] = k.astype(BF)
    v = _dot(h, wm_ref[:, 2048:3072])
    v_ref[...] = v
    v16_ref[...] = v.astype(BF)
    qi_ref[...] = _dot(h, wm_ref[:, 3072:3584]).astype(BF)
    ki = _dot(h, wki_ref[...])
    ki_ref[...] = ki
    ki16_ref[...] = ki.astype(BF)
    w_out_ref[...] = _dot(h, ww_ref[...]) * IDX_SCALE


def _odd_proj(x2d, tm, g_attn, w_in):
    t = x2d.shape[0]
    c0 = 3 * 1024 + HEADS * IDX_DIM
    wm = w_in[:, :c0].astype(BF)
    wki = w_in[:, c0:c0 + IDX_DIM].astype(BF)
    ww = jnp.pad(w_in[:, c0 + IDX_DIM:], ((0, 0), (0, LANES - HEADS))).astype(BF)
    row = lambda w: pl.BlockSpec((tm, w), lambda i: (i, 0))
    out_shapes = [
        jax.ShapeDtypeStruct((t, 1024), BF),
        jax.ShapeDtypeStruct((t, 1024), F32),
        jax.ShapeDtypeStruct((t, 1024), F32),
        jax.ShapeDtypeStruct((t, 1024), BF),
        jax.ShapeDtypeStruct((t, 1024), BF),
        jax.ShapeDtypeStruct((t, HEADS * IDX_DIM), BF),
        jax.ShapeDtypeStruct((t, IDX_DIM), F32),
        jax.ShapeDtypeStruct((t, IDX_DIM), BF),
        jax.ShapeDtypeStruct((t, LANES), F32),
    ]
    return pl.pallas_call(
        _odd_proj_kernel,
        grid=(t // tm,),
        in_specs=[row(D_MODEL), _const_spec((1, D_MODEL)), _const_spec(wm.shape),
                  _const_spec(wki.shape), _const_spec(ww.shape)],
        out_specs=[row(s.shape[1]) for s in out_shapes],
        out_shape=out_shapes,
        compiler_params=_params(("parallel",)),
        name="odd_proj",
    )(x2d, g_attn.reshape(1, -1), wm, wki, ww)


def _indexer_kernel(qi_ref, wt_ref, k_ref, sel_ref, key_ref, hi_ref, cut_ref, *, tq, tk, past, top_k,
                    nk, idx_bits):
    qblk = pl.program_id(1)
    nkv = past // tk + qblk + 1
    q = qi_ref[0]
    wt = wt_ref[0]
    qh = [q[:, h * IDX_DIM:(h + 1) * IDX_DIM] for h in range(HEADS)]
    krow = lax.broadcasted_iota(jnp.int32, (tk, tq), 0)
    qcol = lax.broadcasted_iota(jnp.int32, (tk, tq), 1)
    kf = float(top_k)

    def keys_of(j, diagonal):
        k = k_ref[0, pl.ds(pl.multiple_of(j * tk, tk), tk), :]
        dots = [_dot_nt(k, qh[h]) for h in range(HEADS)]
        sc = jnp.zeros((tk, tq), F32)
        for h in range(HEADS):
            sc = sc + wt[h:h + 1, :] * jnp.maximum(dots[h], 0.0)
        sc = jnp.where(sc == 0.0, 0.0, sc)
        bits = pltpu.bitcast(sc, jnp.int32)
        key = jnp.where(bits < 0, bits ^ 0x7FFFFFFF, bits)
        if diagonal:
            key = jnp.where((krow // CHUNK) <= (qcol // CHUNK), key, INT_MIN)
        return key

    def put_keys(j, diagonal):
        key = keys_of(j, diagonal)
        key_ref[j] = key
        hi_ref[j] = lax.shift_right_arithmetic(key, 16).astype(jnp.int16)

    def score_pair(i, carry):
        put_keys(2 * i, False)
        put_keys(2 * i + 1, False)
        return carry

    n_far = nkv - 1
    lax.fori_loop(0, n_far // 2, score_pair, 0)
    pl.when(n_far % 2 == 1)(functools.partial(put_keys, n_far - 1, False))
    put_keys(nkv - 1, True)

    def count(pred):
        def body(j, acc):
            hit = jnp.where(pred(key_ref[j], j * tk + krow), 1.0, 0.0)
            for r in range(tk // ACC_ROWS):
                acc = acc + hit[r * ACC_ROWS:(r + 1) * ACC_ROWS]
            return acc
        acc = lax.fori_loop(0, nkv, body, jnp.zeros((ACC_ROWS, tq), F32))
        return jnp.sum(acc, axis=0, keepdims=True)

    def count_hi(cand_hi):
        cand16 = cand_hi.astype(jnp.int16)
        def body(j, acc):
            hit = jnp.where(hi_ref[j] >= cand16, jnp.int16(1), jnp.int16(0))
            for r in range(tk // ACC_ROWS):
                acc = acc + hit[r * ACC_ROWS:(r + 1) * ACC_ROWS]
            return acc
        acc = lax.fori_loop(0, nkv, body, jnp.zeros((ACC_ROWS, tq), jnp.int16))
        return jnp.sum(acc.astype(F32), axis=0, keepdims=True)

    c0 = count_hi(jnp.zeros((1, tq), jnp.int32))
    thr_hi = jnp.where(c0 >= kf, 0, -2 ** 15).astype(jnp.int32)

    def hi_body(i, thr_hi):
        cand = thr_hi + lax.shift_left(jnp.int32(1), 14 - i)
        return jnp.where(count_hi(cand) >= kf, cand, thr_hi)

    thr_hi = lax.fori_loop(0, 15, hi_body, thr_hi)

    top = 2 ** 15 - 1
    c_above = jnp.where(thr_hi == top, 0.0, count_hi(jnp.minimum(thr_hi + 1, top)))
    hi16 = thr_hi.astype(jnp.int16)

    def lo_prep(j, carry):
        lo = ((key_ref[j] & 0xFFFF) - 2 ** 15).astype(jnp.int16)
        hi_ref[j] = jnp.where(hi_ref[j] == hi16, lo, jnp.int16(-2 ** 15))
        return carry

    lax.fori_loop(0, nkv, lo_prep, 0)

    def lo_body(i, thr_lo):
        cand = thr_lo + lax.shift_left(jnp.int32(1), 15 - i)
        return jnp.where(c_above + count_hi(cand) >= kf, cand, thr_lo)

    thr_lo = lax.fori_loop(0, 16, lo_body, jnp.full((1, tq), -2 ** 15, jnp.int32))
    thr = lax.shift_left(thr_hi, 16) + (thr_lo + 2 ** 15)

    c_gt = count(lambda key, idx: key > thr)
    c_ge = count(lambda key, idx: key >= thr)
    need = kf - c_gt
    real = thr > KEY_NEG_INF
    excess = real & (c_ge - c_gt > need)
    any_excess = jnp.max(jnp.where(excess, 1.0, 0.0)) > 0.0

    def emit(keep_fn):
        def body(j, carry):
            keep = keep_fn(key_ref[j], j * tk + krow)
            sel_ref[0, 0, j] = jnp.where(keep, 0.0, -jnp.inf).astype(BF)
            return carry
        lax.fori_loop(0, nkv, body, 0)

    @pl.when(jnp.logical_not(any_excess))
    def _():
        thr1 = jnp.maximum(thr, KEY_NEG_INF + 1)
        emit(lambda key, idx: key >= thr1)

    @pl.when(any_excess)
    def _():
        def idx_body(i, cut):
            cand = cut + lax.shift_left(jnp.int32(1), idx_bits - 1 - i)
            c = count(lambda key, idx: (key == thr) & (idx < cand))
            return jnp.where(c < need, cand, cut)
        cut_ref[...] = lax.fori_loop(0, idx_bits, idx_body, jnp.zeros((1, tq), jnp.int32))
        cut = cut_ref[...]
        floor = jnp.maximum(thr, KEY_NEG_INF)
        emit(lambda key, idx: (key > floor) | ((key == thr) & real & (idx <= cut)))

    def zero_body(j, carry):
        sel_ref[0, 0, j] = jnp.zeros((tk, tq), BF)
        return carry

    lax.fori_loop(nkv, nk, zero_body, 0)


def _indexer(qidx, widx, kidx16, *, tq, past, top_k):
    b, sq, _ = qidx.shape
    sk = kidx16.shape[1]
    tk = tq
    nk = sk // tk
    idx_bits = max(1, (sk - 1).bit_length())
    kern = functools.partial(_indexer_kernel, tq=tq, tk=tk, past=past, top_k=top_k, nk=nk,
                             idx_bits=idx_bits)
    widx_t = widx[:, :, :HEADS].transpose(0, 2, 1)
    return pl.pallas_call(
        kern,
        grid=(b, sq // tq),
        in_specs=[pl.BlockSpec((1, tq, HEADS * IDX_DIM), lambda bi, i: (bi, i, 0)),
                  pl.BlockSpec((1, HEADS, tq), lambda bi, i: (bi, 0, i)),
                  pl.BlockSpec((1, sk, IDX_DIM), lambda bi, i: (bi, 0, 0))],
        out_specs=pl.BlockSpec((1, 1, nk, tk, tq), lambda bi, i: (bi, i, 0, 0, 0)),
        out_shape=jax.ShapeDtypeStruct((b, sq // tq, nk, tk, tq), BF),
        scratch_shapes=[pltpu.VMEM((nk, tk, tq), jnp.int32), pltpu.VMEM((nk, tk, tq), jnp.int16),
                        pltpu.VMEM((1, tq), jnp.int32)],
        compiler_params=_params(("parallel", "arbitrary")),
        name="indexer",
    )(qidx, widx_t, kidx16)


def _rope_tables(pos):
    half = MLA_ROPE // 2
    inv = ROPE_THETA ** (-jnp.arange(half, dtype=F32) / half)
    ang = pos.astype(F32)[:, None] * inv[None, :]
    pad = jnp.zeros((pos.shape[0], HEAD_W - MLA_ROPE), F32)
    cos, sin = jnp.cos(ang), jnp.sin(ang)
    return (jnp.concatenate([cos, cos, pad], axis=1), jnp.concatenate([sin, sin, pad], axis=1))


def _row_tile(seq):
    return ROW_TILE if seq % ROW_TILE == 0 else CHUNK


def _pad_seq(a):
    n = -a.shape[1] % ATT_TILE
    return a if n == 0 else jnp.pad(a, ((0, 0), (0, n), (0, 0)))


def _even_layer(x, past_len, past, bias, p, lam_init):
    b, s, _ = x.shape
    tm = _row_tile(s)
    pos = past_len + jnp.arange(s, dtype=jnp.int32)
    cos_p, sin_p = _rope_tables(pos)
    x2d = x.reshape(b * s, D_MODEL)
    qcat, ckv, kpe, dq, dk, dv, dk16, dv16 = _even_proj(
        x2d, s, tm, p["g_attn"], p["w_in"], p["g_q"], p["w_q_up"], p["g_kv"], cos_p, sin_p)
    r3 = lambda a: a.reshape(b, s, -1)
    if past is None:
        ckv_all, kpe_all, dk_all, dv_all = r3(ckv), r3(kpe), r3(dk16), r3(dv16)
    else:
        c_ckv, c_kpe, c_dk, c_dv = past
        ckv_all = jnp.concatenate([c_ckv, r3(ckv)], axis=1)
        kpe_all = jnp.concatenate([c_kpe, r3(kpe)], axis=1)
        dk_all = jnp.concatenate([c_dk.reshape(b, past_len, -1).astype(BF), r3(dk16)], axis=1)
        dv_all = jnp.concatenate([c_dv.reshape(b, past_len, -1).astype(BF), r3(dv16)], axis=1)
    ckv_all, kpe_all, dk_all, dv_all = map(_pad_seq, (ckv_all, kpe_all, dk_all, dv_all))
    sk = ckv_all.shape[1]
    kcat, vmla = _kv_up(ckv_all.reshape(b * sk, KV_LORA), kpe_all.reshape(b * sk, MLA_ROPE),
                        p["w_kv_up"], ROW_TILE if (b * sk) % ROW_TILE == 0 else ATT_TILE)
    a_out = _flash(_pad_seq(r3(qcat)), kcat.reshape(b, sk, -1), vmla.reshape(b, sk, -1),
                   tq=ATT_TILE, past=past_len, dq=MLA_QK)
    lam_params = jnp.stack([p["lq1"], p["lk1"], p["lq2"], p["lk2"]]).astype(F32)
    b_out = _flash(_pad_seq(r3(dq)), dk_all, dv_all, tq=ATT_TILE, past=past_len, dq=HEAD_W, bias=bias,
                   lam_params=lam_params, g_subln=p["g_subln"], lam_init=lam_init)
    parts = [a_out[:, :s].reshape(b * s, -1), b_out[:, :s].reshape(b * s, -1)]
    new = (r3(ckv), r3(kpe), dk.reshape(b, s, HEADS, 2 * DIFF_DIM), dv.reshape(b, s, HEADS, 2 * DIFF_DIM))
    return x2d, parts, new


def _odd_layer(x, past_len, past, bias, p):
    b, s, _ = x.shape
    tm = _row_tile(s)
    x2d = x.reshape(b * s, D_MODEL)
    q, k, v, k16, v16, qi, ki, ki16, widx = _odd_proj(x2d, tm, p["g_attn"], p["w_in"])
    r3 = lambda a: a.reshape(b, s, -1)
    if past is None:
        k_all, v_all, ki_all = r3(k16), r3(v16), r3(ki16)
    else:
        c_k, c_v, c_ki = past
        k_all = jnp.concatenate([c_k.reshape(b, past_len, -1).astype(BF), r3(k16)], axis=1)
        v_all = jnp.concatenate([c_v.reshape(b, past_len, -1).astype(BF), r3(v16)], axis=1)
        ki_all = jnp.concatenate([c_ki.astype(BF), r3(ki16)], axis=1)
    top_k = min(DSA_TOPK, k_all.shape[1] // 4)
    k_all, v_all, ki_all = map(_pad_seq, (k_all, v_all, ki_all))
    sel = _indexer(_pad_seq(r3(qi)), _pad_seq(r3(widx)), ki_all, tq=ATT_TILE, past=past_len, top_k=top_k)
    out = _flash(_pad_seq(r3(q)), k_all, v_all, tq=ATT_TILE, past=past_len, dq=HEAD_W, bias=bias, sel=sel)
    new = (k.reshape(b, s, HEADS, DSA_DIM), v.reshape(b, s, HEADS, DSA_DIM), r3(ki))
    return x2d, [out[:, :s].reshape(b * s, -1)], new


def kernel(x_prompt, x_sample, cache_mla_ckv, cache_mla_kpe, cache_diff_k, cache_diff_v, cache_dsa_k, cache_dsa_v, cache_dsa_kidx, g_attn_even, w_in_even, g_q_lora, w_q_up, g_kv_lora, w_kv_up, lambda_q1, lambda_k1, lambda_q2, lambda_k2, g_diff_subln, w_out_even, g_attn_odd, w_in_odd, w_out_odd, rel_bias_table, g_ffn, w_gate, w_up, w_down, g_final):
    depth = g_ffn.shape[0]
    past_len = cache_mla_ckv.shape[2]
    groups = [
        dict(x=x_prompt, past_len=0, has_past=False),
        dict(x=x_sample, past_len=past_len, has_past=True),
    ]
    bias = _bias_tiles(rel_bias_table, ATT_TILE, ATT_TILE)
    results = []
    for grp in groups:
        x = grp["x"]
        b, s, _ = x.shape
        tm = _row_tile(s)
        even_new, odd_new = [], []
        for layer in range(depth):
            if layer % 2 == 0:
                e = layer // 2
                lam_init = 0.8 - 0.6 * math.exp(-0.3 * layer)
                p = dict(g_attn=g_attn_even[e], w_in=w_in_even[e], g_q=g_q_lora[e], w_q_up=w_q_up[e],
                         g_kv=g_kv_lora[e], w_kv_up=w_kv_up[e], lq1=lambda_q1[e], lk1=lambda_k1[e],
                         lq2=lambda_q2[e], lk2=lambda_k2[e], g_subln=g_diff_subln[e])
                past = ((cache_mla_ckv[e], cache_mla_kpe[e], cache_diff_k[e], cache_diff_v[e])
                        if grp["has_past"] else None)
                x2d, parts, new = _even_layer(x, grp["past_len"], past, bias, p, lam_init)
                even_new.append(new)
                w_out = w_out_even[e]
            else:
                o = layer // 2
                p = dict(g_attn=g_attn_odd[o], w_in=w_in_odd[o])
                past = ((cache_dsa_k[o], cache_dsa_v[o], cache_dsa_kidx[o])
                        if grp["has_past"] else None)
                x2d, parts, new = _odd_layer(x, grp["past_len"], past, bias, p)
                odd_new.append(new)
                w_out = w_out_odd[o]
            gfin = g_final if layer == depth - 1 else None
            x = _post(x2d, parts, w_out, g_ffn[layer], w_gate[layer], w_up[layer], w_down[layer],
                      gfin, tm).reshape(b, s, D_MODEL)
        results.append((x, [jnp.stack(a) for a in zip(*even_new)], [jnp.stack(a) for a in zip(*odd_new)]))
    (yp, ep, op), (ys, es, os_) = results
    return (yp, ys, *ep, *op, *es, *os_)
```
